```python
import jax, jax.numpy as jnp
from jax import lax
import numpy as np

D_MODEL = 2048
BATCH = 8
SEQ = 2048
DEPTH = 2

N_MIXERS = 2
N_LAYERS_A = (DEPTH + 1) // 2
N_LAYERS_B = DEPTH // 2

CHUNK = 128
GM_WIDTH = D_MODEL
GM_GROUPS = 16
GM_GROUP_DIM = GM_WIDTH // GM_GROUPS

HEAD_DIM = 128
N_HEADS = D_MODEL // HEAD_DIM
N_KV_HEADS = 4
N_REP = N_HEADS // N_KV_HEADS
Q_BLOCK = 128
GRID_W = 64
AXIS_DIM = HEAD_DIM // 2
ROPE_THETA = 10000.0

D_FF = 4 * D_MODEL

NORM_EPS = 1e-6

kernel_name = "hybrid_gmlp_axial_gqa_encoder"


def rmsnorm(x, g):
    xf = x.astype(jnp.float32)
    xf = xf * lax.rsqrt(jnp.mean(xf * xf, axis=-1, keepdims=True) + NORM_EPS)
    return (xf * g.astype(jnp.float32)).astype(x.dtype)


def layernorm(x, g, b):
    xf = x.astype(jnp.float32)
    mu = jnp.mean(xf, axis=-1, keepdims=True)
    xc = xf - mu
    var = jnp.mean(xc * xc, axis=-1, keepdims=True)
    y = xc * lax.rsqrt(var + NORM_EPS) * g.astype(jnp.float32) + b.astype(jnp.float32)
    return y.astype(x.dtype)


def gmlp_mixer(h, w_in, ln_g, ln_b, w_s, b_s, w_out):
    B, S, _ = h.shape
    z = jax.nn.gelu(h @ w_in, approximate=False)
    u, v = z[..., :GM_WIDTH], z[..., GM_WIDTH:]
    v = layernorm(v, ln_g, ln_b)
    v = v.reshape(B, S // CHUNK, CHUNK, GM_GROUPS, GM_GROUP_DIM)
    s = jnp.einsum("gpq,bnqgc->bnpgc", w_s.astype(v.dtype), v)
    s = s + b_s.T.astype(v.dtype)[None, None, :, :, None]
    s = s.reshape(B, S, GM_WIDTH)
    return (u * s) @ w_out


def rope_axis(x, pos):
    half = AXIS_DIM // 2
    inv_freq = ROPE_THETA ** (-jnp.arange(0, AXIS_DIM, 2, dtype=jnp.float32) / AXIS_DIM)
    ang = pos[:, None] * inv_freq[None, :]
    cos, sin = jnp.cos(ang), jnp.sin(ang)
    xf = x.astype(jnp.float32)
    x1, x2 = xf[..., :half], xf[..., half:]
    out = jnp.concatenate([x1 * cos - x2 * sin, x2 * cos + x1 * sin], axis=-1)
    return out.astype(x.dtype)


def axial_rope(x, row, col):
    xt = jnp.swapaxes(x, 1, 2)
    xr = rope_axis(xt[..., :AXIS_DIM], row)
    xc = rope_axis(xt[..., AXIS_DIM:], col)
    return jnp.swapaxes(jnp.concatenate([xr, xc], axis=-1), 1, 2)


def attention_mixer(h, w_qkv, q_norm, k_norm, w_o):
    B, S, _ = h.shape
    ROWS = S // GRID_W
    row = jnp.repeat(jnp.arange(ROWS, dtype=jnp.float32), GRID_W)
    col = jnp.tile(jnp.arange(GRID_W, dtype=jnp.float32), ROWS)

    qkv = h @ w_qkv
    q_w = N_HEADS * HEAD_DIM
    kv_w = N_KV_HEADS * HEAD_DIM
    q = qkv[..., :q_w].reshape(B, S, N_HEADS, HEAD_DIM)
    k = qkv[..., q_w:q_w + kv_w].reshape(B, S, N_KV_HEADS, HEAD_DIM)
    v = qkv[..., q_w + kv_w:].reshape(B, S, N_KV_HEADS, HEAD_DIM)

    q = axial_rope(rmsnorm(q, q_norm), row, col)
    k = axial_rope(rmsnorm(k, k_norm), row, col)

    n_blocks = S // Q_BLOCK
    q = q.reshape(B, n_blocks, Q_BLOCK, N_KV_HEADS, N_REP, HEAD_DIM)
    q_blocks = jnp.transpose(q, (1, 0, 3, 4, 2, 5))
    k = jnp.transpose(k, (0, 2, 1, 3))
    v = jnp.transpose(v, (0, 2, 1, 3))
    scale = HEAD_DIM ** -0.5

    def one_block(qb):
        s = jnp.einsum("bgrqd,bgkd->bgrqk", qb, k).astype(jnp.float32) * scale
        p = jax.nn.softmax(s, axis=-1)
        return jnp.einsum("bgrqk,bgkd->bgrqd", p.astype(v.dtype), v)

    o = lax.map(one_block, q_blocks)
    o = jnp.transpose(o, (1, 0, 4, 2, 3, 5)).reshape(B, S, N_HEADS * HEAD_DIM)
    return o @ w_o


def sqrelu_mlp(h, w1, w2):
    a = jax.nn.relu(h @ w1)
    return (a * a) @ w2


def setup_inputs(seed: int = 0) -> dict:
    key = jax.random.key(seed)
    ks = jax.random.split(key, 20)
    f32 = jnp.float32

    def nrm(k, shape, scale):
        return jax.random.normal(k, shape, f32) * scale

    x = jax.random.normal(ks[0], (BATCH, SEQ, D_MODEL), f32)

    gm_w_in = nrm(ks[1], (N_LAYERS_A, D_MODEL, 2 * GM_WIDTH), D_MODEL ** -0.5)
    gm_ln_g = 1.0 + nrm(ks[2], (N_LAYERS_A, GM_WIDTH), 0.02)
    gm_ln_b = nrm(ks[3], (N_LAYERS_A, GM_WIDTH), 0.02)
    gm_w_s = nrm(ks[4], (N_LAYERS_A, GM_GROUPS, CHUNK, CHUNK), CHUNK ** -0.5)
    gm_b_s = 1.0 + nrm(ks[5], (N_LAYERS_A, GM_GROUPS, CHUNK), 0.02)
    gm_w_out = nrm(ks[6], (N_LAYERS_A, GM_WIDTH, D_MODEL), GM_WIDTH ** -0.5)

    qkv_w = (N_HEADS + 2 * N_KV_HEADS) * HEAD_DIM
    attn_w_qkv = nrm(ks[7], (N_LAYERS_B, D_MODEL, qkv_w), D_MODEL ** -0.5)
    attn_q_norm = 1.0 + nrm(ks[8], (N_LAYERS_B, HEAD_DIM), 0.02)
    attn_k_norm = 1.0 + nrm(ks[9], (N_LAYERS_B, HEAD_DIM), 0.02)
    attn_w_o = nrm(ks[10], (N_LAYERS_B, N_HEADS * HEAD_DIM, D_MODEL), (N_HEADS * HEAD_DIM) ** -0.5)

    ffn_w1 = nrm(ks[11], (DEPTH, D_MODEL, D_FF), D_MODEL ** -0.5)
    ffn_w2 = nrm(ks[12], (DEPTH, D_FF, D_MODEL), D_FF ** -0.5)

    norm_mix = 1.0 + nrm(ks[13], (DEPTH, D_MODEL), 0.02)
    norm_ffn = 1.0 + nrm(ks[14], (DEPTH, D_MODEL), 0.02)
    norm_final = 1.0 + nrm(ks[15], (D_MODEL,), 0.02)

    return {
        "x": x,
        "gm_w_in": gm_w_in, "gm_ln_g": gm_ln_g, "gm_ln_b": gm_ln_b,
        "gm_w_s": gm_w_s, "gm_b_s": gm_b_s, "gm_w_out": gm_w_out,
        "attn_w_qkv": attn_w_qkv, "attn_q_norm": attn_q_norm,
        "attn_k_norm": attn_k_norm, "attn_w_o": attn_w_o,
        "ffn_w1": ffn_w1, "ffn_w2": ffn_w2,
        "norm_mix": norm_mix, "norm_ffn": norm_ffn, "norm_final": norm_final,
    }


def reference(x, gm_w_in, gm_ln_g, gm_ln_b, gm_w_s, gm_b_s, gm_w_out,
              attn_w_qkv, attn_q_norm, attn_k_norm, attn_w_o,
              ffn_w1, ffn_w2, norm_mix, norm_ffn, norm_final):
    ia = 0
    ib = 0
    for i in range(DEPTH):
        h = rmsnorm(x, norm_mix[i])
        if i % N_MIXERS == 0:
            x = x + gmlp_mixer(h, gm_w_in[ia], gm_ln_g[ia], gm_ln_b[ia],
                               gm_w_s[ia], gm_b_s[ia], gm_w_out[ia])
            ia += 1
        else:
            x = x + attention_mixer(h, attn_w_qkv[ib], attn_q_norm[ib],
                                    attn_k_norm[ib], attn_w_o[ib])
            ib += 1
        h = rmsnorm(x, norm_ffn[i])
        x = x + sqrelu_mlp(h, ffn_w1[i], ffn_w2[i])
    return rmsnorm(x, norm_final)
```

```python
import functools
import math

import jax
import jax.numpy as jnp
from jax import lax
from jax.experimental import pallas as pl
from jax.experimental.pallas import tpu as pltpu

NORM_EPS = 1e-6
CHUNK = 128
HEAD_DIM = 128
N_KV_HEADS = 4
GRID_W = 64
ROPE_THETA = 10000.0

V7X_VMEM_BYTES = 64 * 1024 * 1024
VMEM_RESERVE_BYTES = 6 * 1024 * 1024
LANES = 128

BF16 = jnp.bfloat16
F32 = jnp.float32


def _vmem_limit(estimate_bytes):
    return int(min(V7X_VMEM_BYTES - VMEM_RESERVE_BYTES, estimate_bytes))


def _rmsnorm(xf, g):
    ms = jnp.mean(xf * xf, axis=-1, keepdims=True)
    return xf * lax.rsqrt(ms + NORM_EPS) * g


def _gelu(z):
    return 0.5 * z * (1.0 + lax.erf(z * math.sqrt(0.5)))


def _mm(a, b):
    return jnp.dot(a, b, preferred_element_type=F32)


def _gmlp_in_kernel(x_ref, g_ref, w_ref, lg_ref, lb_ref, u_ref, v_ref, *, width):
    h = _rmsnorm(x_ref[...], g_ref[...]).astype(BF16)
    u_ref[...] = _gelu(_mm(h, w_ref[:, :width])).astype(BF16)
    v = _gelu(_mm(h, w_ref[:, width:]))
    mu = jnp.mean(v, axis=-1, keepdims=True)
    vc = v - mu
    var = jnp.mean(vc * vc, axis=-1, keepdims=True)
    vn = vc * lax.rsqrt(var + NORM_EPS) * lg_ref[...] + lb_ref[...]
    v_ref[...] = vn.astype(BF16)


def _gmlp_in(x2, g, w_in, ln_g, ln_b, *, bm):
    m, d = x2.shape
    width = w_in.shape[1] // 2
    row = lambda i: (i, 0)
    fixed = lambda i: (0, 0)
    est = (2 * bm * d * 4 + w_in.size * 2 + 2 * 2 * bm * width * 2
           + bm * d * 2 + 3 * bm * width * 4)
    return pl.pallas_call(
        functools.partial(_gmlp_in_kernel, width=width),
        grid=(m // bm,),
        in_specs=[
            pl.BlockSpec((bm, d), row),
            pl.BlockSpec((1, d), fixed),
            pl.BlockSpec(w_in.shape, fixed),
            pl.BlockSpec((1, width), fixed),
            pl.BlockSpec((1, width), fixed),
        ],
        out_specs=[pl.BlockSpec((bm, width), row), pl.BlockSpec((bm, width), row)],
        out_shape=[jax.ShapeDtypeStruct((m, width), BF16)] * 2,
        compiler_params=pltpu.CompilerParams(
            dimension_semantics=("parallel",), vmem_limit_bytes=_vmem_limit(est + (8 << 20))),
        name="gmlp_in",
    )(x2, g, w_in, ln_g, ln_b)


def _gmlp_out_kernel(u_ref, v_ref, ws_ref, bs_ref, wout_ref, x_ref, o_ref, y_ref, *, groups):
    bm = u_ref.shape[0]
    n_chunks = bm // CHUNK
    for g in range(groups):
        cols = slice(g * LANES, (g + 1) * LANES)
        rhs = jnp.concatenate(
            [v_ref[c * CHUNK:(c + 1) * CHUNK, cols] for c in range(n_chunks)], axis=1)
        sg = _mm(ws_ref[g], rhs)
        bias = bs_ref[:, cols]
        for c in range(n_chunks):
            rows = slice(c * CHUNK, (c + 1) * CHUNK)
            s = sg[:, c * LANES:(c + 1) * LANES] + bias
            y_ref[rows, cols] = (u_ref[rows, cols].astype(F32) * s).astype(BF16)
    o_ref[...] = x_ref[...] + _mm(y_ref[...], wout_ref[...])


def _gmlp_out(u, v, w_s, b_full, w_out, x2, *, bm):
    m, width = u.shape
    d = w_out.shape[1]
    groups = w_s.shape[0]
    row = lambda i: (i, 0)
    est = (2 * 2 * bm * width * 2 + w_s.size * 2 + b_full.size * 4 + w_out.size * 2
           + 2 * 2 * bm * d * 4 + bm * width * 2 + bm * d * 4)
    return pl.pallas_call(
        functools.partial(_gmlp_out_kernel, groups=groups),
        grid=(m // bm,),
        in_specs=[
            pl.BlockSpec((bm, width), row),
            pl.BlockSpec((bm, width), row),
            pl.BlockSpec(w_s.shape, lambda i: (0, 0, 0)),
            pl.BlockSpec(b_full.shape, lambda i: (0, 0)),
            pl.BlockSpec(w_out.shape, lambda i: (0, 0)),
            pl.BlockSpec((bm, d), row),
        ],
        out_specs=pl.BlockSpec((bm, d), row),
        out_shape=jax.ShapeDtypeStruct((m, d), F32),
        scratch_shapes=[pltpu.VMEM((bm, width), BF16)],
        compiler_params=pltpu.CompilerParams(
            dimension_semantics=("parallel",), vmem_limit_bytes=_vmem_limit(est + (8 << 20))),
        name="gmlp_out",
    )(u, v, w_s, b_full, w_out, x2)


def _ffn_kernel(x_ref, g_ref, w1_ref, w2_ref, gf_ref, o_ref, h_ref, *, final_norm):
    j = pl.program_id(1)

    @pl.when(j == 0)
    def _():
        xf = x_ref[...]
        h_ref[...] = _rmsnorm(xf, g_ref[...]).astype(BF16)
        o_ref[...] = xf

    a = jnp.maximum(_mm(h_ref[...], w1_ref[...]), 0.0)
    o_ref[...] += _mm((a * a).astype(BF16), w2_ref[...])

    if final_norm:
        @pl.when(j == pl.num_programs(1) - 1)
        def _():
            o_ref[...] = _rmsnorm(o_ref[...], gf_ref[...])


def _ffn(x2, g, w1, w2, g_final, *, bm, fc, final_norm):
    m, d = x2.shape
    d_ff = w1.shape[1]
    est = (2 * bm * d * 4 + 2 * d * fc * 2 + 2 * fc * d * 2 + 2 * bm * d * 4
           + bm * d * 2 + bm * fc * 4 + bm * fc * 2 + bm * d * 4)
    return pl.pallas_call(
        functools.partial(_ffn_kernel, final_norm=final_norm),
        grid=(m // bm, d_ff // fc),
        in_specs=[
            pl.BlockSpec((bm, d), lambda i, j: (i, 0)),
            pl.BlockSpec((1, d), lambda i, j: (0, 0)),
            pl.BlockSpec((d, fc), lambda i, j: (0, j)),
            pl.BlockSpec((fc, d), lambda i, j: (j, 0)),
            pl.BlockSpec((1, d), lambda i, j: (0, 0)),
        ],
        out_specs=pl.BlockSpec((bm, d), lambda i, j: (i, 0)),
        out_shape=jax.ShapeDtypeStruct((m, d), F32),
        scratch_shapes=[pltpu.VMEM((bm, d), BF16)],
        compiler_params=pltpu.CompilerParams(
            dimension_semantics=("parallel", "arbitrary"),
            vmem_limit_bytes=_vmem_limit(est + (8 << 20))),
        name="ffn_final" if final_norm else "ffn",
    )(x2, g, w1, w2, g_final)


def _qk_norm_rope(t, gain, cos, sin_signed, lower_half):
    tn = _rmsnorm(t, gain)
    half = HEAD_DIM // 4
    partner = jnp.where(lower_half,
                        pltpu.roll(tn, HEAD_DIM - half, axis=1),
                        pltpu.roll(tn, half, axis=1))
    return tn * cos + partner * sin_signed


def _qkv_kernel(x_ref, g_ref, w_ref, qn_ref, kn_ref, cos_ref, sin_ref,
                q_ref, k_ref, v_ref, *, n_heads, n_kv, scale):
    h = _rmsnorm(x_ref[...], g_ref[...]).astype(BF16)
    cos = cos_ref[...]
    sin_signed = sin_ref[...]
    lane = lax.broadcasted_iota(jnp.int32, cos.shape, 1)
    lower_half = (lane % (HEAD_DIM // 2)) < (HEAD_DIM // 4)
    q_w = n_heads * HEAD_DIM
    kv_w = n_kv * HEAD_DIM
    q = _mm(h, w_ref[:, :q_w])
    for hd in range(n_heads):
        cols = slice(hd * HEAD_DIM, (hd + 1) * HEAD_DIM)
        qh = _qk_norm_rope(q[:, cols], qn_ref[...], cos, sin_signed, lower_half)
        q_ref[:, cols] = (qh * scale).astype(BF16)
    k = _mm(h, w_ref[:, q_w:q_w + kv_w])
    for hd in range(n_kv):
        cols = slice(hd * HEAD_DIM, (hd + 1) * HEAD_DIM)
        k_ref[:, cols] = _qk_norm_rope(k[:, cols], kn_ref[...], cos, sin_signed,
                                       lower_half).astype(BF16)
    v_ref[...] = _mm(h, w_ref[:, q_w + kv_w:]).astype(BF16)


def _qkv(x2, g, w_qkv, q_norm, k_norm, cos, sin_signed, *, bm, seq):
    m, d = x2.shape
    n_kv = N_KV_HEADS
    kv_w = n_kv * HEAD_DIM
    q_w = w_qkv.shape[1] - 2 * kv_w
    n_heads = q_w // HEAD_DIM
    row = lambda i: (i, 0)
    fixed = lambda i: (0, 0)
    pos = lambda i: (i % (seq // bm), 0)
    est = (2 * bm * d * 4 + w_qkv.size * 2 + 4 * bm * HEAD_DIM * 4
           + 2 * bm * (q_w + 2 * kv_w) * 2 + bm * d * 2 + bm * q_w * 4 * 2)
    return pl.pallas_call(
        functools.partial(_qkv_kernel, n_heads=n_heads, n_kv=n_kv, scale=HEAD_DIM ** -0.5),
        grid=(m // bm,),
        in_specs=[
            pl.BlockSpec((bm, d), row),
            pl.BlockSpec((1, d), fixed),
            pl.BlockSpec(w_qkv.shape, fixed),
            pl.BlockSpec((1, HEAD_DIM), fixed),
            pl.BlockSpec((1, HEAD_DIM), fixed),
            pl.BlockSpec((bm, HEAD_DIM), pos),
            pl.BlockSpec((bm, HEAD_DIM), pos),
        ],
        out_specs=[pl.BlockSpec((bm, q_w), row), pl.BlockSpec((bm, kv_w), row),
                   pl.BlockSpec((bm, kv_w), row)],
        out_shape=[jax.ShapeDtypeStruct((m, q_w), BF16),
                   jax.ShapeDtypeStruct((m, kv_w), BF16),
                   jax.ShapeDtypeStruct((m, kv_w), BF16)],
        compiler_params=pltpu.CompilerParams(
            dimension_semantics=("parallel",), vmem_limit_bytes=_vmem_limit(est + (8 << 20))),
        name="attn_qkv",
    )(x2, g, w_qkv, q_norm, k_norm, cos, sin_signed)


def _attn_kernel(q_ref, k_ref, v_ref, o_ref, *, n_rep):
    qb = q_ref.shape[0]
    q = jnp.concatenate(
        [q_ref[:, r * HEAD_DIM:(r + 1) * HEAD_DIM] for r in range(n_rep)], axis=0)
    s = lax.dot_general(q, k_ref[...], (((1,), (1,)), ((), ())),
                        preferred_element_type=F32)
    p = jnp.exp(s - jnp.max(s, axis=-1, keepdims=True))
    l = jnp.sum(p, axis=-1, keepdims=True)
    o = _mm(p.astype(BF16), v_ref[...]) / l
    for r in range(n_rep):
        o_ref[:, r * HEAD_DIM:(r + 1) * HEAD_DIM] = o[r * qb:(r + 1) * qb].astype(BF16)


def _attention(q, k, v, *, batch, seq, qb):
    m, q_w = q.shape
    n_kv = k.shape[1] // HEAD_DIM
    n_rep = q_w // HEAD_DIM // n_kv
    nb = seq // qb
    est = (2 * 2 * qb * n_rep * HEAD_DIM * 2 + 2 * 2 * seq * HEAD_DIM * 2
           + 3 * n_rep * qb * seq * 4)
    return pl.pallas_call(
        functools.partial(_attn_kernel, n_rep=n_rep),
        grid=(batch, n_kv, nb),
        in_specs=[
            pl.BlockSpec((qb, n_rep * HEAD_DIM), lambda b, g, i: (b * nb + i, g)),
            pl.BlockSpec((seq, HEAD_DIM), lambda b, g, i: (b, g)),
            pl.BlockSpec((seq, HEAD_DIM), lambda b, g, i: (b, g)),
        ],
        out_specs=pl.BlockSpec((qb, n_rep * HEAD_DIM), lambda b, g, i: (b * nb + i, g)),
        out_shape=jax.ShapeDtypeStruct((m, q_w), BF16),
        compiler_params=pltpu.CompilerParams(
            dimension_semantics=("parallel", "parallel", "parallel"),
            vmem_limit_bytes=_vmem_limit(est + (8 << 20))),
        name="attn_core",
    )(q, k, v)


def _proj_kernel(a_ref, w_ref, x_ref, o_ref):
    o_ref[...] = x_ref[...] + _mm(a_ref[...], w_ref[...])


def _proj_residual(a, w, x2, *, bm):
    m, kdim = a.shape
    d = w.shape[1]
    row = lambda i: (i, 0)
    est = 2 * bm * kdim * 2 + w.size * 2 + 2 * 2 * bm * d * 4 + bm * d * 4
    return pl.pallas_call(
        _proj_kernel,
        grid=(m // bm,),
        in_specs=[pl.BlockSpec((bm, kdim), row), pl.BlockSpec(w.shape, lambda i: (0, 0)),
                  pl.BlockSpec((bm, d), row)],
        out_specs=pl.BlockSpec((bm, d), row),
        out_shape=jax.ShapeDtypeStruct((m, d), F32),
        compiler_params=pltpu.CompilerParams(
            dimension_semantics=("parallel",), vmem_limit_bytes=_vmem_limit(est + (8 << 20))),
        name="attn_out",
    )(a, w, x2)


def _rope_tables(seq):
    axis_dim = HEAD_DIM // 2
    half = axis_dim // 2
    t = jnp.arange(seq)
    row = (t // GRID_W).astype(F32)
    col = (t % GRID_W).astype(F32)
    inv_freq = ROPE_THETA ** (-jnp.arange(0, axis_dim, 2, dtype=F32) / axis_dim)

    def axis_tables(pos):
        ang = pos[:, None] * inv_freq[None, :]
        c, s = jnp.cos(ang), jnp.sin(ang)
        return jnp.concatenate([c, c], axis=-1), jnp.concatenate([-s, s], axis=-1)

    cr, sr = axis_tables(row)
    cc, sc = axis_tables(col)
    assert cr.shape == (seq, 2 * half)
    return jnp.concatenate([cr, cc], axis=-1), jnp.concatenate([sr, sc], axis=-1)


def kernel(x, gm_w_in, gm_ln_g, gm_ln_b, gm_w_s, gm_b_s, gm_w_out, attn_w_qkv, attn_q_norm, attn_k_norm, attn_w_o, ffn_w1, ffn_w2, norm_mix, norm_ffn, norm_final):
    batch, seq, d = x.shape
    depth = norm_mix.shape[0]
    m = batch * seq
    x2 = x.reshape(m, d)
    cos, sin_signed = _rope_tables(seq)
    bm = 512
    ia = ib = 0
    for i in range(depth):
        g_mix = norm_mix[i].reshape(1, d)
        if i % 2 == 0:
            width = gm_w_out.shape[1]
            u, v = _gmlp_in(x2, g_mix, gm_w_in[ia].astype(BF16),
                            gm_ln_g[ia].reshape(1, width), gm_ln_b[ia].reshape(1, width), bm=bm)
            b_full = jnp.repeat(gm_b_s[ia].T, width // gm_b_s.shape[1], axis=1)
            x2 = _gmlp_out(u, v, gm_w_s[ia].astype(BF16), b_full,
                           gm_w_out[ia].astype(BF16), x2, bm=bm)
            ia += 1
        else:
            q, k, v = _qkv(x2, g_mix, attn_w_qkv[ib].astype(BF16),
                           attn_q_norm[ib].reshape(1, HEAD_DIM),
                           attn_k_norm[ib].reshape(1, HEAD_DIM), cos, sin_signed,
                           bm=bm, seq=seq)
            o = _attention(q, k, v, batch=batch, seq=seq, qb=128)
            x2 = _proj_residual(o, attn_w_o[ib].astype(BF16), x2, bm=bm)
            ib += 1
        x2 = _ffn(x2, norm_ffn[i].reshape(1, d), ffn_w1[i].astype(BF16), ffn_w2[i].astype(BF16),
                  norm_final.reshape(1, d), bm=bm, fc=1024, final_norm=(i == depth - 1))
    return x2.reshape(batch, seq, d)
```

```python
import functools
import math

import jax
import jax.numpy as jnp
from jax import lax
from jax.experimental import pallas as pl
from jax.experimental.pallas import tpu as pltpu

NORM_EPS = 1e-6
CHUNK = 128
HEAD_DIM = 128
N_KV_HEADS = 4
GRID_W = 64
ROPE_THETA = 10000.0

V7X_VMEM_BYTES = 64 * 1024 * 1024
VMEM_RESERVE_BYTES = 6 * 1024 * 1024
LANES = 128

BF16 = jnp.bfloat16
F32 = jnp.float32


def _vmem_limit(estimate_bytes):
    return int(min(V7X_VMEM_BYTES - VMEM_RESERVE_BYTES, estimate_bytes))


def _rmsnorm(xf, g):
    ms = jnp.mean(xf * xf, axis=-1, keepdims=True)
    return xf * lax.rsqrt(ms + NORM_EPS) * g


def _gelu(z):
    return 0.5 * z * (1.0 + lax.erf(z * math.sqrt(0.5)))


def _mm(a, b):
    return jnp.dot(a, b, preferred_element_type=F32)


def _gmlp_in_kernel(x_ref, g_ref, w_ref, lg_ref, lb_ref, u_ref, v_ref, *, width):
    h = _rmsnorm(x_ref[...], g_ref[...]).astype(BF16)
    v = _gelu(_mm(h, w_ref[:, width:]))
    mu = jnp.mean(v, axis=-1, keepdims=True)
    vc = v - mu
    var = jnp.mean(vc * vc, axis=-1, keepdims=True)
    vn = vc * lax.rsqrt(var + NORM_EPS) * lg_ref[...] + lb_ref[...]
    v_ref[...] = vn.astype(BF16)
    u_ref[...] = _gelu(_mm(h, w_ref[:, :width])).astype(BF16)


def _gmlp_in(x2, g, w_in, ln_g, ln_b, *, bm):
    m, d = x2.shape
    width = w_in.shape[1] // 2
    row = lambda i: (i, 0)
    fixed = lambda i: (0, 0)
    est = (2 * bm * d * 4 + w_in.size * 2 + 2 * 2 * bm * width * 2
           + bm * d * 2 + 3 * bm * width * 4)
    return pl.pallas_call(
        functools.partial(_gmlp_in_kernel, width=width),
        grid=(m // bm,),
        in_specs=[
            pl.BlockSpec((bm, d), row),
            pl.BlockSpec((1, d), fixed),
            pl.BlockSpec(w_in.shape, fixed),
            pl.BlockSpec((1, width), fixed),
            pl.BlockSpec((1, width), fixed),
        ],
        out_specs=[pl.BlockSpec((bm, width), row), pl.BlockSpec((bm, width), row)],
        out_shape=[jax.ShapeDtypeStruct((m, width), BF16)] * 2,
        compiler_params=pltpu.CompilerParams(
            dimension_semantics=("parallel",), vmem_limit_bytes=_vmem_limit(est + (8 << 20))),
        name="gmlp_in",
    )(x2, g, w_in, ln_g, ln_b)


def _gmlp_out_kernel(u_ref, v_ref, ws_ref, bs_ref, wout_ref, x_ref, o_ref, y_ref, *, groups):
    bm = u_ref.shape[0]
    n_chunks = bm // CHUNK
    for g in range(groups):
        cols = slice(g * LANES, (g + 1) * LANES)
        rhs = jnp.concatenate(
            [v_ref[c * CHUNK:(c + 1) * CHUNK, cols] for c in range(n_chunks)], axis=1)
        sg = _mm(ws_ref[g], rhs)
        bias = bs_ref[:, cols]
        for c in range(n_chunks):
            rows = slice(c * CHUNK, (c + 1) * CHUNK)
            s = sg[:, c * LANES:(c + 1) * LANES] + bias
            y_ref[rows, cols] = (u_ref[rows, cols].astype(F32) * s).astype(BF16)
    o_ref[...] = x_ref[...] + _mm(y_ref[...], wout_ref[...])


def _gmlp_out(u, v, w_s, b_full, w_out, x2, *, bm):
    m, width = u.shape
    d = w_out.shape[1]
    groups = w_s.shape[0]
    row = lambda i: (i, 0)
    est = (2 * 2 * bm * width * 2 + w_s.size * 2 + b_full.size * 4 + w_out.size * 2
           + 2 * 2 * bm * d * 4 + bm * width * 2 + bm * d * 4)
    return pl.pallas_call(
        functools.partial(_gmlp_out_kernel, groups=groups),
        grid=(m // bm,),
        in_specs=[
            pl.BlockSpec((bm, width), row),
            pl.BlockSpec((bm, width), row),
            pl.BlockSpec(w_s.shape, lambda i: (0, 0, 0)),
            pl.BlockSpec(b_full.shape, lambda i: (0, 0)),
            pl.BlockSpec(w_out.shape, lambda i: (0, 0)),
            pl.BlockSpec((bm, d), row),
        ],
        out_specs=pl.BlockSpec((bm, d), row),
        out_shape=jax.ShapeDtypeStruct((m, d), F32),
        scratch_shapes=[pltpu.VMEM((bm, width), BF16)],
        compiler_params=pltpu.CompilerParams(
            dimension_semantics=("parallel",), vmem_limit_bytes=_vmem_limit(est + (8 << 20))),
        name="gmlp_out",
    )(u, v, w_s, b_full, w_out, x2)


def _ffn_kernel(x_ref, g_ref, w1_ref, w2_ref, gf_ref, o_ref, h_ref, *, final_norm):
    j = pl.program_id(1)

    @pl.when(j == 0)
    def _():
        xf = x_ref[...]
        h_ref[...] = _rmsnorm(xf, g_ref[...]).astype(BF16)
        o_ref[...] = xf

    a = jnp.maximum(_mm(h_ref[...], w1_ref[...]), 0.0)
    o_ref[...] += _mm((a * a).astype(BF16), w2_ref[...])

    if final_norm:
        @pl.when(j == pl.num_programs(1) - 1)
        def _():
            o_ref[...] = _rmsnorm(o_ref[...], gf_ref[...])


def _ffn(x2, g, w1, w2, g_final, *, bm, fc, final_norm):
    m, d = x2.shape
    d_ff = w1.shape[1]
    est = (2 * bm * d * 4 + 2 * d * fc * 2 + 2 * fc * d * 2 + 2 * bm * d * 4
           + bm * d * 2 + bm * fc * 4 + bm * fc * 2 + bm * d * 4)
    return pl.pallas_call(
        functools.partial(_ffn_kernel, final_norm=final_norm),
        grid=(m // bm, d_ff // fc),
        in_specs=[
            pl.BlockSpec((bm, d), lambda i, j: (i, 0)),
            pl.BlockSpec((1, d), lambda i, j: (0, 0)),
            pl.BlockSpec((d, fc), lambda i, j: (0, j)),
            pl.BlockSpec((fc, d), lambda i, j: (j, 0)),
            pl.BlockSpec((1, d), lambda i, j: (0, 0)),
        ],
        out_specs=pl.BlockSpec((bm, d), lambda i, j: (i, 0)),
        out_shape=jax.ShapeDtypeStruct((m, d), F32),
        scratch_shapes=[pltpu.VMEM((bm, d), BF16)],
        compiler_params=pltpu.CompilerParams(
            dimension_semantics=("parallel", "arbitrary"),
            vmem_limit_bytes=_vmem_limit(est + (8 << 20))),
        name="ffn_final" if final_norm else "ffn",
    )(x2, g, w1, w2, g_final)


def _qk_norm_rope(t, gain, cos, sin_signed, lower_half):
    tn = _rmsnorm(t, gain)
    half = HEAD_DIM // 4
    partner = jnp.where(lower_half,
                        pltpu.roll(tn, HEAD_DIM - half, axis=1),
                        pltpu.roll(tn, half, axis=1))
    return tn * cos + partner * sin_signed


def _qkv_kernel(x_ref, g_ref, w_ref, qn_ref, kn_ref, cos_ref, sin_ref,
                q_ref, k_ref, v_ref, *, n_heads, n_kv, scale):
    h = _rmsnorm(x_ref[...], g_ref[...]).astype(BF16)
    cos = cos_ref[...]
    sin_signed = sin_ref[...]
    lane = lax.broadcasted_iota(jnp.int32, cos.shape, 1)
    lower_half = (lane % (HEAD_DIM // 2)) < (HEAD_DIM // 4)
    q_w = n_heads * HEAD_DIM
    kv_w = n_kv * HEAD_DIM

    def head_group(out_ref, w_col0, out_col0, gain, out_scale):
        t = _mm(h, w_ref[:, w_col0:w_col0 + kv_w])
        for hd in range(n_kv):
            cols = slice(hd * HEAD_DIM, (hd + 1) * HEAD_DIM)
            th = _qk_norm_rope(t[:, cols], gain, cos, sin_signed, lower_half)
            if out_scale != 1.0:
                th = th * out_scale
            out_ref[:, out_col0 + hd * HEAD_DIM:out_col0 + (hd + 1) * HEAD_DIM] = th.astype(BF16)

    head_group(k_ref, q_w, 0, kn_ref[...], 1.0)
    for c0 in range(0, q_w, kv_w):
        head_group(q_ref, c0, c0, qn_ref[...], scale)
    v_ref[...] = _mm(h, w_ref[:, q_w + kv_w:]).astype(BF16)


def _qkv(x2, g, w_qkv, q_norm, k_norm, cos, sin_signed, *, bm, seq):
    m, d = x2.shape
    n_kv = N_KV_HEADS
    kv_w = n_kv * HEAD_DIM
    q_w = w_qkv.shape[1] - 2 * kv_w
    n_heads = q_w // HEAD_DIM
    row = lambda i: (i, 0)
    fixed = lambda i: (0, 0)
    pos = lambda i: (i % (seq // bm), 0)
    est = (2 * bm * d * 4 + w_qkv.size * 2 + 4 * bm * HEAD_DIM * 4
           + 2 * bm * (q_w + 2 * kv_w) * 2 + bm * d * 2 + bm * q_w * 4 * 2)
    return pl.pallas_call(
        functools.partial(_qkv_kernel, n_heads=n_heads, n_kv=n_kv,
                          scale=HEAD_DIM ** -0.5 * math.log2(math.e)),
        grid=(m // bm,),
        in_specs=[
            pl.BlockSpec((bm, d), row),
            pl.BlockSpec((1, d), fixed),
            pl.BlockSpec(w_qkv.shape, fixed),
            pl.BlockSpec((1, HEAD_DIM), fixed),
            pl.BlockSpec((1, HEAD_DIM), fixed),
            pl.BlockSpec((bm, HEAD_DIM), pos),
            pl.BlockSpec((bm, HEAD_DIM), pos),
        ],
        out_specs=[pl.BlockSpec((bm, q_w), row), pl.BlockSpec((bm, kv_w), row),
                   pl.BlockSpec((bm, kv_w), row)],
        out_shape=[jax.ShapeDtypeStruct((m, q_w), BF16),
                   jax.ShapeDtypeStruct((m, kv_w), BF16),
                   jax.ShapeDtypeStruct((m, kv_w), BF16)],
        compiler_params=pltpu.CompilerParams(
            dimension_semantics=("parallel",), vmem_limit_bytes=_vmem_limit(est + (8 << 20))),
        name="attn_qkv",
    )(x2, g, w_qkv, q_norm, k_norm, cos, sin_signed)


def _attn_kernel(q_ref, k_ref, v_ref, o_ref, *, n_rep):
    for r in range(n_rep):
        cols = slice(r * HEAD_DIM, (r + 1) * HEAD_DIM)
        s = lax.dot_general(q_ref[:, cols], k_ref[...], (((1,), (1,)), ((), ())),
                            preferred_element_type=F32)
        p = jnp.exp2(s - jnp.max(s, axis=-1, keepdims=True))
        l = jnp.sum(p, axis=-1, keepdims=True)
        o_ref[:, cols] = (_mm(p.astype(BF16), v_ref[...]) / l).astype(BF16)


def _attention(q, k, v, *, batch, seq, qb):
    m, q_w = q.shape
    n_kv = k.shape[1] // HEAD_DIM
    n_rep = q_w // HEAD_DIM // n_kv
    nb = seq // qb
    est = (2 * 2 * qb * n_rep * HEAD_DIM * 2 + 2 * 2 * seq * HEAD_DIM * 2
           + 3 * n_rep * qb * seq * 4)
    return pl.pallas_call(
        functools.partial(_attn_kernel, n_rep=n_rep),
        grid=(batch, n_kv, nb),
        in_specs=[
            pl.BlockSpec((qb, n_rep * HEAD_DIM), lambda b, g, i: (b * nb + i, g)),
            pl.BlockSpec((seq, HEAD_DIM), lambda b, g, i: (b, g)),
            pl.BlockSpec((seq, HEAD_DIM), lambda b, g, i: (b, g)),
        ],
        out_specs=pl.BlockSpec((qb, n_rep * HEAD_DIM), lambda b, g, i: (b * nb + i, g)),
        out_shape=jax.ShapeDtypeStruct((m, q_w), BF16),
        compiler_params=pltpu.CompilerParams(
            dimension_semantics=("parallel", "parallel", "parallel"),
            vmem_limit_bytes=_vmem_limit(est + (8 << 20))),
        name="attn_core",
    )(q, k, v)


def _proj_kernel(a_ref, w_ref, x_ref, o_ref):
    o_ref[...] = x_ref[...] + _mm(a_ref[...], w_ref[...])


def _proj_residual(a, w, x2, *, bm):
    m, kdim = a.shape
    d = w.shape[1]
    row = lambda i: (i, 0)
    est = 2 * bm * kdim * 2 + w.size * 2 + 2 * 2 * bm * d * 4 + bm * d * 4
    return pl.pallas_call(
        _proj_kernel,
        grid=(m // bm,),
        in_specs=[pl.BlockSpec((bm, kdim), row), pl.BlockSpec(w.shape, lambda i: (0, 0)),
                  pl.BlockSpec((bm, d), row)],
        out_specs=pl.BlockSpec((bm, d), row),
        out_shape=jax.ShapeDtypeStruct((m, d), F32),
        compiler_params=pltpu.CompilerParams(
            dimension_semantics=("parallel",), vmem_limit_bytes=_vmem_limit(est + (8 << 20))),
        name="attn_out",
    )(a, w, x2)


def _rope_tables(seq):
    axis_dim = HEAD_DIM // 2
    half = axis_dim // 2
    t = jnp.arange(seq)
    row = (t // GRID_W).astype(F32)
    col = (t % GRID_W).astype(F32)
    inv_freq = ROPE_THETA ** (-jnp.arange(0, axis_dim, 2, dtype=F32) / axis_dim)

    def axis_tables(pos):
        ang = pos[:, None] * inv_freq[None, :]
        c, s = jnp.cos(ang), jnp.sin(ang)
        return jnp.concatenate([c, c], axis=-1), jnp.concatenate([-s, s], axis=-1)

    cr, sr = axis_tables(row)
    cc, sc = axis_tables(col)
    assert cr.shape == (seq, 2 * half)
    return jnp.concatenate([cr, cc], axis=-1), jnp.concatenate([sr, sc], axis=-1)


def kernel(x, gm_w_in, gm_ln_g, gm_ln_b, gm_w_s, gm_b_s, gm_w_out, attn_w_qkv, attn_q_norm, attn_k_norm, attn_w_o, ffn_w1, ffn_w2, norm_mix, norm_ffn, norm_final):
    batch, seq, d = x.shape
    depth = norm_mix.shape[0]
    m = batch * seq
    x2 = x.reshape(m, d)
    cos, sin_signed = _rope_tables(seq)
    bm = 512
    ia = ib = 0
    for i in range(depth):
        g_mix = norm_mix[i].reshape(1, d)
        if i % 2 == 0:
            width = gm_w_out.shape[1]
            u, v = _gmlp_in(x2, g_mix, gm_w_in[ia].astype(BF16),
                            gm_ln_g[ia].reshape(1, width), gm_ln_b[ia].reshape(1, width), bm=bm)
            b_full = jnp.repeat(gm_b_s[ia].T, width // gm_b_s.shape[1], axis=1)
            x2 = _gmlp_out(u, v, gm_w_s[ia].astype(BF16), b_full,
                           gm_w_out[ia].astype(BF16), x2, bm=bm)
            ia += 1
        else:
            q, k, v = _qkv(x2, g_mix, attn_w_qkv[ib].astype(BF16),
                           attn_q_norm[ib].reshape(1, HEAD_DIM),
                           attn_k_norm[ib].reshape(1, HEAD_DIM), cos, sin_signed,
                           bm=bm, seq=seq)
            o = _attention(q, k, v, batch=batch, seq=seq, qb=512)
            x2 = _proj_residual(o, attn_w_o[ib].astype(BF16), x2, bm=bm)
            ib += 1
        x2 = _ffn(x2, norm_ffn[i].reshape(1, d), ffn_w1[i].astype(BF16), ffn_w2[i].astype(BF16),
                  norm_final.reshape(1, d), bm=bm, fc=1024, final_norm=(i == depth - 1))
    return x2.reshape(batch, seq, d)
```

```python
import functools
import math

import jax
import jax.numpy as jnp
from jax import lax
from jax.experimental import pallas as pl
from jax.experimental.pallas import tpu as pltpu

NORM_EPS = 1e-6
CHUNK = 128
HEAD_DIM = 128
N_KV_HEADS = 4
GRID_W = 64
ROPE_THETA = 10000.0

V7X_VMEM_BYTES = 64 * 1024 * 1024
VMEM_RESERVE_BYTES = 6 * 1024 * 1024
VMEM_TEMP_BYTES = 8 * 1024 * 1024
LANES = 128

BF16 = jnp.bfloat16
F32 = jnp.float32


def _vmem_limit(estimate_bytes):
    return int(min(V7X_VMEM_BYTES - VMEM_RESERVE_BYTES, estimate_bytes + VMEM_TEMP_BYTES))


def _rmsnorm(xf, g):
    ms = jnp.mean(xf * xf, axis=-1, keepdims=True)
    return xf * lax.rsqrt(ms + NORM_EPS) * g


def _gelu(z):
    return 0.5 * z * (1.0 + lax.erf(z * math.sqrt(0.5)))


def _mm(a, b):
    return jnp.dot(a, b, preferred_element_type=F32)


def _row_tiled_call(body, *, name, n_steps, in_specs, out_specs, out_shape, args,
                    est, casts=(), scratch_shapes=()):
    n_in, n_out, n_cast = len(in_specs), len(out_specs), len(casts)

    def kernel(*refs):
        ins, refs = refs[:n_in], refs[n_in:]
        cast_in, refs = refs[:n_cast], refs[n_cast:]
        outs, refs = refs[:n_out], refs[n_out:]
        cast_out, scratch = refs[:n_cast], refs[n_cast:]
        body(*ins, *outs, *scratch)
        for src, dst in zip(cast_in, cast_out):
            dst[...] = src[...].astype(BF16)

    cast_specs = [pl.BlockSpec((w.shape[0] // n_steps, w.shape[1]), lambda i: (i, 0))
                  for w in casts]
    est += sum(2 * (w.size // n_steps) * (4 + 2) for w in casts)
    outs = pl.pallas_call(
        kernel,
        grid=(n_steps,),
        in_specs=list(in_specs) + cast_specs,
        out_specs=list(out_specs) + cast_specs,
        out_shape=list(out_shape) + [jax.ShapeDtypeStruct(w.shape, BF16) for w in casts],
        scratch_shapes=list(scratch_shapes),
        compiler_params=pltpu.CompilerParams(
            dimension_semantics=("parallel",), vmem_limit_bytes=_vmem_limit(est)),
        name=name,
    )(*args, *casts)
    return outs[:n_out], outs[n_out:]


def _gmlp_in_kernel(x_ref, g_ref, w_ref, lg_ref, lb_ref, u_ref, v_ref, *, width):
    h = _rmsnorm(x_ref[...], g_ref[...]).astype(BF16)
    v = _gelu(_mm(h, w_ref[:, width:]))
    mu = jnp.mean(v, axis=-1, keepdims=True)
    vc = v - mu
    var = jnp.mean(vc * vc, axis=-1, keepdims=True)
    vn = vc * lax.rsqrt(var + NORM_EPS) * lg_ref[...] + lb_ref[...]
    v_ref[...] = vn.astype(BF16)
    u_ref[...] = _gelu(_mm(h, w_ref[:, :width])).astype(BF16)


def _gmlp_in(x2, g, w_in, ln_g, ln_b, *, bm, casts):
    m, d = x2.shape
    width = w_in.shape[1] // 2
    row = lambda i: (i, 0)
    fixed = lambda i: (0, 0)
    est = (2 * bm * d * 4 + w_in.size * 2 + 2 * 2 * bm * width * 2
           + bm * d * 2 + 3 * bm * width * 4)
    return _row_tiled_call(
        functools.partial(_gmlp_in_kernel, width=width),
        name="gmlp_in", n_steps=m // bm,
        in_specs=[
            pl.BlockSpec((bm, d), row),
            pl.BlockSpec((1, d), fixed),
            pl.BlockSpec(w_in.shape, fixed),
            pl.BlockSpec((1, width), fixed),
            pl.BlockSpec((1, width), fixed),
        ],
        out_specs=[pl.BlockSpec((bm, width), row), pl.BlockSpec((bm, width), row)],
        out_shape=[jax.ShapeDtypeStruct((m, width), BF16)] * 2,
        args=(x2, g, w_in, ln_g, ln_b), est=est, casts=casts)


def _gmlp_out_kernel(u_ref, v_ref, ws_ref, bs_ref, wout_ref, x_ref, o_ref, y_ref, *, groups):
    bm = u_ref.shape[0]
    n_chunks = bm // CHUNK
    for g in range(groups):
        cols = slice(g * LANES, (g + 1) * LANES)
        rhs = jnp.concatenate(
            [v_ref[c * CHUNK:(c + 1) * CHUNK, cols] for c in range(n_chunks)], axis=1)
        sg = _mm(ws_ref[g], rhs)
        bias = bs_ref[:, cols]
        for c in range(n_chunks):
            rows = slice(c * CHUNK, (c + 1) * CHUNK)
            s = sg[:, c * LANES:(c + 1) * LANES] + bias
            y_ref[rows, cols] = (u_ref[rows, cols].astype(F32) * s).astype(BF16)
    o_ref[...] = x_ref[...] + _mm(y_ref[...], wout_ref[...])


def _gmlp_out(u, v, w_s, b_full, w_out, x2, *, bm, casts):
    m, width = u.shape
    d = w_out.shape[1]
    groups = w_s.shape[0]
    row = lambda i: (i, 0)
    est = (2 * 2 * bm * width * 2 + w_s.size * 2 + b_full.size * 4 + w_out.size * 2
           + 2 * 2 * bm * d * 4 + bm * width * 2 + bm * d * 4)
    return _row_tiled_call(
        functools.partial(_gmlp_out_kernel, groups=groups),
        name="gmlp_out", n_steps=m // bm,
        in_specs=[
            pl.BlockSpec((bm, width), row),
            pl.BlockSpec((bm, width), row),
            pl.BlockSpec(w_s.shape, lambda i: (0, 0, 0)),
            pl.BlockSpec(b_full.shape, lambda i: (0, 0)),
            pl.BlockSpec(w_out.shape, lambda i: (0, 0)),
            pl.BlockSpec((bm, d), row),
        ],
        out_specs=[pl.BlockSpec((bm, d), row)],
        out_shape=[jax.ShapeDtypeStruct((m, d), F32)],
        scratch_shapes=[pltpu.VMEM((bm, width), BF16)],
        args=(u, v, w_s, b_full, w_out, x2), est=est, casts=casts)


def _ffn_kernel(x_ref, g_ref, w1_ref, w2_ref, gf_ref, o_ref, h_ref, *, final_norm):
    j = pl.program_id(1)

    @pl.when(j == 0)
    def _():
        xf = x_ref[...]
        h_ref[...] = _rmsnorm(xf, g_ref[...]).astype(BF16)
        o_ref[...] = xf

    a = jnp.maximum(_mm(h_ref[...], w1_ref[...]), 0.0)
    o_ref[...] += _mm((a * a).astype(BF16), w2_ref[...])

    if final_norm:
        @pl.when(j == pl.num_programs(1) - 1)
        def _():
            o_ref[...] = _rmsnorm(o_ref[...], gf_ref[...])


def _ffn(x2, g, w1, w2, g_final, *, bm, fc, final_norm):
    m, d = x2.shape
    d_ff = w1.shape[1]
    est = (2 * bm * d * 4 + 2 * d * fc * 2 + 2 * fc * d * 2 + 2 * bm * d * 4
           + bm * d * 2 + bm * fc * 4 + bm * fc * 2 + bm * d * 4)
    return pl.pallas_call(
        functools.partial(_ffn_kernel, final_norm=final_norm),
        grid=(m // bm, d_ff // fc),
        in_specs=[
            pl.BlockSpec((bm, d), lambda i, j: (i, 0)),
            pl.BlockSpec((1, d), lambda i, j: (0, 0)),
            pl.BlockSpec((d, fc), lambda i, j: (0, j)),
            pl.BlockSpec((fc, d), lambda i, j: (j, 0)),
            pl.BlockSpec((1, d), lambda i, j: (0, 0)),
        ],
        out_specs=pl.BlockSpec((bm, d), lambda i, j: (i, 0)),
        out_shape=jax.ShapeDtypeStruct((m, d), F32),
        scratch_shapes=[pltpu.VMEM((bm, d), BF16)],
        compiler_params=pltpu.CompilerParams(
            dimension_semantics=("parallel", "arbitrary"),
            vmem_limit_bytes=_vmem_limit(est)),
        name="ffn_final" if final_norm else "ffn",
    )(x2, g, w1, w2, g_final)


def _qk_norm_rope(t, gain, cos, sin_signed, lower_half):
    tn = _rmsnorm(t, gain)
    half = HEAD_DIM // 4
    partner = jnp.where(lower_half,
                        pltpu.roll(tn, HEAD_DIM - half, axis=1),
                        pltpu.roll(tn, half, axis=1))
    return tn * cos + partner * sin_signed


def _qkv_kernel(x_ref, g_ref, w_ref, qn_ref, kn_ref, cos_ref, sin_ref,
                q_ref, k_ref, v_ref, *, n_heads, n_kv, scale):
    h = _rmsnorm(x_ref[...], g_ref[...]).astype(BF16)
    cos = cos_ref[...]
    sin_signed = sin_ref[...]
    lane = lax.broadcasted_iota(jnp.int32, cos.shape, 1)
    lower_half = (lane % (HEAD_DIM // 2)) < (HEAD_DIM // 4)
    q_w = n_heads * HEAD_DIM
    kv_w = n_kv * HEAD_DIM

    def head_group(out_ref, w_col0, out_col0, gain, out_scale):
        t = _mm(h, w_ref[:, w_col0:w_col0 + kv_w])
        for hd in range(n_kv):
            cols = slice(hd * HEAD_DIM, (hd + 1) * HEAD_DIM)
            th = _qk_norm_rope(t[:, cols], gain, cos, sin_signed, lower_half)
            if out_scale != 1.0:
                th = th * out_scale
            out_ref[:, out_col0 + hd * HEAD_DIM:out_col0 + (hd + 1) * HEAD_DIM] = th.astype(BF16)

    head_group(k_ref, q_w, 0, kn_ref[...], 1.0)
    for c0 in range(0, q_w, kv_w):
        head_group(q_ref, c0, c0, qn_ref[...], scale)
    v_ref[...] = _mm(h, w_ref[:, q_w + kv_w:]).astype(BF16)


def _qkv(x2, g, w_qkv, q_norm, k_norm, cos, sin_signed, *, bm, seq, casts):
    m, d = x2.shape
    n_kv = N_KV_HEADS
    kv_w = n_kv * HEAD_DIM
    q_w = w_qkv.shape[1] - 2 * kv_w
    n_heads = q_w // HEAD_DIM
    row = lambda i: (i, 0)
    fixed = lambda i: (0, 0)
    pos = lambda i: (i % (seq // bm), 0)
    est = (2 * bm * d * 4 + w_qkv.size * 2 + 4 * bm * HEAD_DIM * 4
           + 2 * bm * (q_w + 2 * kv_w) * 2 + bm * d * 2 + bm * q_w * 4 * 2)
    return _row_tiled_call(
        functools.partial(_qkv_kernel, n_heads=n_heads, n_kv=n_kv,
                          scale=HEAD_DIM ** -0.5 * math.log2(math.e)),
        name="attn_qkv", n_steps=m // bm,
        in_specs=[
            pl.BlockSpec((bm, d), row),
            pl.BlockSpec((1, d), fixed),
            pl.BlockSpec(w_qkv.shape, fixed),
            pl.BlockSpec((1, HEAD_DIM), fixed),
            pl.BlockSpec((1, HEAD_DIM), fixed),
            pl.BlockSpec((bm, HEAD_DIM), pos),
            pl.BlockSpec((bm, HEAD_DIM), pos),
        ],
        out_specs=[pl.BlockSpec((bm, q_w), row), pl.BlockSpec((bm, kv_w), row),
                   pl.BlockSpec((bm, kv_w), row)],
        out_shape=[jax.ShapeDtypeStruct((m, q_w), BF16),
                   jax.ShapeDtypeStruct((m, kv_w), BF16),
                   jax.ShapeDtypeStruct((m, kv_w), BF16)],
        args=(x2, g, w_qkv, q_norm, k_norm, cos, sin_signed), est=est, casts=casts)


def _attn_kernel(q_ref, k_ref, v_ref, o_ref, *, n_rep, rows):
    qb = q_ref.shape[0]
    for r in range(n_rep):
        cols = slice(r * HEAD_DIM, (r + 1) * HEAD_DIM)
        for r0 in range(0, qb, rows):
            rs = slice(r0, r0 + rows)
            s = lax.dot_general(q_ref[rs, cols], k_ref[...], (((1,), (1,)), ((), ())),
                                preferred_element_type=F32)
            p = jnp.exp2(s - jnp.max(s, axis=-1, keepdims=True))
            l = jnp.sum(p, axis=-1, keepdims=True)
            o_ref[rs, cols] = (_mm(p.astype(BF16), v_ref[...]) / l).astype(BF16)


def _attention(q, k, v, *, batch, seq, qb, rows):
    m, q_w = q.shape
    n_kv = k.shape[1] // HEAD_DIM
    n_rep = q_w // HEAD_DIM // n_kv
    nb = seq // qb
    est = (2 * 2 * qb * n_rep * HEAD_DIM * 2 + 2 * 2 * seq * HEAD_DIM * 2
           + 4 * rows * seq * (4 + 2))
    return pl.pallas_call(
        functools.partial(_attn_kernel, n_rep=n_rep, rows=rows),
        grid=(batch, n_kv, nb),
        in_specs=[
            pl.BlockSpec((qb, n_rep * HEAD_DIM), lambda b, g, i: (b * nb + i, g)),
            pl.BlockSpec((seq, HEAD_DIM), lambda b, g, i: (b, g)),
            pl.BlockSpec((seq, HEAD_DIM), lambda b, g, i: (b, g)),
        ],
        out_specs=pl.BlockSpec((qb, n_rep * HEAD_DIM), lambda b, g, i: (b * nb + i, g)),
        out_shape=jax.ShapeDtypeStruct((m, q_w), BF16),
        compiler_params=pltpu.CompilerParams(
            dimension_semantics=("parallel", "parallel", "parallel"),
            vmem_limit_bytes=_vmem_limit(est)),
        name="attn_core",
    )(q, k, v)


def _proj_kernel(a_ref, w_ref, x_ref, o_ref):
    o_ref[...] = x_ref[...] + _mm(a_ref[...], w_ref[...])


def _proj_residual(a, w, x2, *, bm, casts):
    m, kdim = a.shape
    d = w.shape[1]
    row = lambda i: (i, 0)
    est = 2 * bm * kdim * 2 + w.size * 2 + 2 * 2 * bm * d * 4 + bm * d * 4
    return _row_tiled_call(
        _proj_kernel, name="attn_out", n_steps=m // bm,
        in_specs=[pl.BlockSpec((bm, kdim), row), pl.BlockSpec(w.shape, lambda i: (0, 0)),
                  pl.BlockSpec((bm, d), row)],
        out_specs=[pl.BlockSpec((bm, d), row)],
        out_shape=[jax.ShapeDtypeStruct((m, d), F32)],
        args=(a, w, x2), est=est, casts=casts)


def _rope_tables(seq):
    axis_dim = HEAD_DIM // 2
    half = axis_dim // 2
    t = jnp.arange(seq)
    row = (t // GRID_W).astype(F32)
    col = (t % GRID_W).astype(F32)
    inv_freq = ROPE_THETA ** (-jnp.arange(0, axis_dim, 2, dtype=F32) / axis_dim)

    def axis_tables(pos):
        ang = pos[:, None] * inv_freq[None, :]
        c, s = jnp.cos(ang), jnp.sin(ang)
        return jnp.concatenate([c, c], axis=-1), jnp.concatenate([-s, s], axis=-1)

    cr, sr = axis_tables(row)
    cc, sc = axis_tables(col)
    assert cr.shape == (seq, 2 * half)
    return jnp.concatenate([cr, cc], axis=-1), jnp.concatenate([sr, sc], axis=-1)


def kernel(x, gm_w_in, gm_ln_g, gm_ln_b, gm_w_s, gm_b_s, gm_w_out, attn_w_qkv, attn_q_norm, attn_k_norm, attn_w_o, ffn_w1, ffn_w2, norm_mix, norm_ffn, norm_final):
    batch, seq, d = x.shape
    assert norm_mix.shape[0] == 2, "two layers: gMLP mixer then attention mixer"
    m = batch * seq
    width = gm_w_out.shape[1]
    x2 = x.reshape(m, d)
    cos, sin_signed = _rope_tables(seq)
    bm = 512
    fc = 1024

    (u, v), (w_out, w1_0) = _gmlp_in(
        x2, norm_mix[0].reshape(1, d), gm_w_in[0].astype(BF16),
        gm_ln_g[0].reshape(1, width), gm_ln_b[0].reshape(1, width), bm=bm,
        casts=(gm_w_out[0], ffn_w1[0]))
    b_full = jnp.repeat(gm_b_s[0].T, width // gm_b_s.shape[1], axis=1)
    (x2,), (w2_0, w_qkv) = _gmlp_out(
        u, v, gm_w_s[0].astype(BF16), b_full, w_out, x2, bm=bm,
        casts=(ffn_w2[0], attn_w_qkv[0]))
    x2 = _ffn(x2, norm_ffn[0].reshape(1, d), w1_0, w2_0, norm_final.reshape(1, d),
              bm=bm, fc=fc, final_norm=False)

    (q, k, v), (w_o, w1_1) = _qkv(
        x2, norm_mix[1].reshape(1, d), w_qkv, attn_q_norm[0].reshape(1, HEAD_DIM),
        attn_k_norm[0].reshape(1, HEAD_DIM), cos, sin_signed, bm=bm, seq=seq,
        casts=(attn_w_o[0], ffn_w1[1]))
    o = _attention(q, k, v, batch=batch, seq=seq, qb=1024, rows=512)
    (x2,), (w2_1,) = _proj_residual(o, w_o, x2, bm=bm, casts=(ffn_w2[1],))
    x2 = _ffn(x2, norm_ffn[1].reshape(1, d), w1_1, w2_1, norm_final.reshape(1, d),
              bm=bm, fc=fc, final_norm=True)
    return x2.reshape(batch, seq, d)
```

```python
import functools
import math
from typing import Callable, NamedTuple, Optional

import jax
import jax.numpy as jnp
from jax import lax
from jax.experimental import pallas as pl
from jax.experimental.pallas import tpu as pltpu

NORM_EPS = 1e-6
CHUNK = 128
HEAD_DIM = 128
N_KV_HEADS = 4
GRID_W = 64
ROPE_THETA = 10000.0

V7X_VMEM_BYTES = 64 * 1024 * 1024
VMEM_RESERVE_BYTES = 6 * 1024 * 1024
VMEM_TEMP_BYTES = 8 * 1024 * 1024
LANES = 128

BF16 = jnp.bfloat16
F32 = jnp.float32


def _vmem_limit(estimate_bytes):
    return int(min(V7X_VMEM_BYTES - VMEM_RESERVE_BYTES, estimate_bytes + VMEM_TEMP_BYTES))


def _rmsnorm(xf, g):
    ms = jnp.mean(xf * xf, axis=-1, keepdims=True)
    return xf * lax.rsqrt(ms + NORM_EPS) * g


def _gelu(z):
    return 0.5 * z * (1.0 + lax.erf(z * math.sqrt(0.5)))


def _mm(a, b):
    return jnp.dot(a, b, preferred_element_type=F32)


class _Cast(NamedTuple):
    w: jax.Array
    layer: int
    transform: Optional[Callable] = None


def _row_tiled_call(body, *, name, n_steps, in_specs, out_specs, out_shape, args,
                    est, casts=(), scratch_shapes=()):
    n_in, n_out, n_cast = len(in_specs), len(out_specs), len(casts)
    cast_layers = [c.layer for c in casts]
    cast_shapes = [c.w.shape[1:] for c in casts]
    cast_fns = [c.transform for c in casts]
    casts = [c.w.reshape(-1, c.w.shape[2]) for c in casts]

    def kernel(*refs):
        ins, refs = refs[:n_in], refs[n_in:]
        cast_in, refs = refs[:n_cast], refs[n_cast:]
        outs, refs = refs[:n_out], refs[n_out:]
        cast_out, scratch = refs[:n_cast], refs[n_cast:]
        body(*ins, *outs, *scratch)
        for src, dst, fn in zip(cast_in, cast_out, cast_fns):
            if fn is None:
                dst[...] = src[...].astype(BF16)
            else:
                fn(src, dst)

    cast_in_specs = [
        pl.BlockSpec((r // n_steps, c), functools.partial(lambda i, layer: (layer * n_steps + i, 0),
                                                          layer=layer))
        for (r, c), layer in zip(cast_shapes, cast_layers)]
    cast_out_specs = [pl.BlockSpec((r // n_steps, c), lambda i: (i, 0)) for r, c in cast_shapes]
    est += sum(2 * (r // n_steps) * c * (4 + 2) for r, c in cast_shapes)
    outs = pl.pallas_call(
        kernel,
        grid=(n_steps,),
        in_specs=list(in_specs) + cast_in_specs,
        out_specs=list(out_specs) + cast_out_specs,
        out_shape=list(out_shape) + [jax.ShapeDtypeStruct(s, BF16) for s in cast_shapes],
        scratch_shapes=list(scratch_shapes),
        compiler_params=pltpu.CompilerParams(
            dimension_semantics=("parallel",), vmem_limit_bytes=_vmem_limit(est)),
        name=name,
    )(*args, *casts)
    return outs[:n_out], outs[n_out:]


def _gmlp_in_kernel(x_ref, g_ref, w_ref, lg_ref, lb_ref, u_ref, v_ref, *, width):
    h = _rmsnorm(x_ref[...], g_ref[...]).astype(BF16)
    v = _gelu(_mm(h, w_ref[:, width:]))
    mu = jnp.mean(v, axis=-1, keepdims=True)
    vc = v - mu
    var = jnp.mean(vc * vc, axis=-1, keepdims=True)
    vn = vc * lax.rsqrt(var + NORM_EPS) * lg_ref[...] + lb_ref[...]
    v_ref[...] = vn.astype(BF16)
    u_ref[...] = _gelu(_mm(h, w_ref[:, :width])).astype(BF16)


def _gmlp_in(x2, g, w_in, ln_g, ln_b, *, bm, casts):
    m, d = x2.shape
    width = w_in.shape[1] // 2
    row = lambda i: (i, 0)
    fixed = lambda i: (0, 0)
    est = (2 * bm * d * 4 + w_in.size * 2 + 2 * 2 * bm * width * 2
           + bm * d * 2 + 3 * bm * width * 4)
    return _row_tiled_call(
        functools.partial(_gmlp_in_kernel, width=width),
        name="gmlp_in", n_steps=m // bm,
        in_specs=[
            pl.BlockSpec((bm, d), row),
            pl.BlockSpec((1, d), fixed),
            pl.BlockSpec(w_in.shape, fixed),
            pl.BlockSpec((1, width), fixed),
            pl.BlockSpec((1, width), fixed),
        ],
        out_specs=[pl.BlockSpec((bm, width), row), pl.BlockSpec((bm, width), row)],
        out_shape=[jax.ShapeDtypeStruct((m, width), BF16)] * 2,
        args=(x2, g, w_in, ln_g, ln_b), est=est, casts=casts)


def _gmlp_out_kernel(u_ref, v_ref, ws_ref, bs_ref, wout_ref, x_ref, o_ref, y_ref, *, groups):
    bm = u_ref.shape[0]
    n_chunks = bm // CHUNK
    for g in range(groups):
        cols = slice(g * LANES, (g + 1) * LANES)
        rhs = jnp.concatenate(
            [v_ref[c * CHUNK:(c + 1) * CHUNK, cols] for c in range(n_chunks)], axis=1)
        sg = _mm(ws_ref[g], rhs)
        bias = bs_ref[:, cols]
        for c in range(n_chunks):
            rows = slice(c * CHUNK, (c + 1) * CHUNK)
            s = sg[:, c * LANES:(c + 1) * LANES] + bias
            y_ref[rows, cols] = (u_ref[rows, cols].astype(F32) * s).astype(BF16)
    o_ref[...] = x_ref[...] + _mm(y_ref[...], wout_ref[...])


def _gmlp_out(u, v, w_s, b_full, w_out, x2, *, bm, casts):
    m, width = u.shape
    d = w_out.shape[1]
    groups = w_s.shape[0]
    row = lambda i: (i, 0)
    est = (2 * 2 * bm * width * 2 + w_s.size * 2 + b_full.size * 4 + w_out.size * 2
           + 2 * 2 * bm * d * 4 + bm * width * 2 + bm * d * 4)
    return _row_tiled_call(
        functools.partial(_gmlp_out_kernel, groups=groups),
        name="gmlp_out", n_steps=m // bm,
        in_specs=[
            pl.BlockSpec((bm, width), row),
            pl.BlockSpec((bm, width), row),
            pl.BlockSpec(w_s.shape, lambda i: (0, 0, 0)),
            pl.BlockSpec(b_full.shape, lambda i: (0, 0)),
            pl.BlockSpec(w_out.shape, lambda i: (0, 0)),
            pl.BlockSpec((bm, d), row),
        ],
        out_specs=[pl.BlockSpec((bm, d), row)],
        out_shape=[jax.ShapeDtypeStruct((m, d), F32)],
        scratch_shapes=[pltpu.VMEM((bm, width), BF16)],
        args=(u, v, w_s, b_full, w_out, x2), est=est, casts=casts)


def _ffn_kernel(x_ref, g_ref, w1_ref, w2_ref, gf_ref, o_ref, h_ref, *, final_norm):
    j = pl.program_id(1)

    def chunk(h, acc):
        a = jnp.maximum(_mm(h, w1_ref[...]), 0.0)
        o_ref[...] = acc + _mm((a * a).astype(BF16), w2_ref[...])

    @pl.when(j == 0)
    def _():
        xf = x_ref[...]
        h_ref[...] = _rmsnorm(xf, g_ref[...]).astype(BF16)
        chunk(h_ref[...], xf)

    @pl.when(j > 0)
    def _():
        chunk(h_ref[...], o_ref[...])

    if final_norm:
        @pl.when(j == pl.num_programs(1) - 1)
        def _():
            o_ref[...] = _rmsnorm(o_ref[...], gf_ref[...])


def _ffn(x2, g, w1, w2, g_final, *, bm, fc, final_norm):
    m, d = x2.shape
    d_ff = w1.shape[1]
    est = (2 * bm * d * 4 + 2 * d * fc * 2 + 2 * fc * d * 2 + 2 * bm * d * 4
           + bm * d * 2 + bm * fc * 4 + bm * fc * 2 + bm * d * 4)
    return pl.pallas_call(
        functools.partial(_ffn_kernel, final_norm=final_norm),
        grid=(m // bm, d_ff // fc),
        in_specs=[
            pl.BlockSpec((bm, d), lambda i, j: (i, 0)),
            pl.BlockSpec((1, d), lambda i, j: (0, 0)),
            pl.BlockSpec((d, fc), lambda i, j: (0, j)),
            pl.BlockSpec((fc, d), lambda i, j: (j, 0)),
            pl.BlockSpec((1, d), lambda i, j: (0, 0)),
        ],
        out_specs=pl.BlockSpec((bm, d), lambda i, j: (i, 0)),
        out_shape=jax.ShapeDtypeStruct((m, d), F32),
        scratch_shapes=[pltpu.VMEM((bm, d), BF16)],
        compiler_params=pltpu.CompilerParams(
            dimension_semantics=("parallel", "arbitrary"),
            vmem_limit_bytes=_vmem_limit(est)),
        name="ffn_final" if final_norm else "ffn",
    )(x2, g, w1, w2, g_final)


QUARTER = HEAD_DIM // 4


def _swap_mid_quarters(t):
    lane = lax.broadcasted_iota(jnp.int32, t.shape, 1)
    from_right = (lane >= QUARTER) & (lane < 2 * QUARTER)
    from_left = (lane >= 2 * QUARTER) & (lane < 3 * QUARTER)
    return jnp.where(from_right, pltpu.roll(t, HEAD_DIM - QUARTER, axis=1),
                     jnp.where(from_left, pltpu.roll(t, QUARTER, axis=1), t))


def _cast_qkv_weights(src_ref, dst_ref, *, qk_width):
    for c0 in range(0, qk_width, HEAD_DIM):
        cols = slice(c0, c0 + HEAD_DIM)
        dst_ref[:, cols] = _swap_mid_quarters(src_ref[:, cols]).astype(BF16)
    dst_ref[:, qk_width:] = src_ref[:, qk_width:].astype(BF16)


def _qkv_kernel(x_ref, g_ref, w_ref, qn_ref, kn_ref, cos_ref, sin_ref,
                q_ref, k_ref, v_ref, *, n_heads, n_kv, scale):
    h = _rmsnorm(x_ref[...], g_ref[...]).astype(BF16)
    cos = cos_ref[...]
    sin_signed = sin_ref[...]
    q_w = n_heads * HEAD_DIM
    kv_w = n_kv * HEAD_DIM

    def partner(t):
        return pltpu.roll(t, HEAD_DIM // 2, axis=1)

    def gained_tables(gain):
        gain_b = _swap_mid_quarters(jnp.broadcast_to(gain, cos.shape))
        return gain_b * cos, partner(gain_b) * sin_signed

    def head_group(out_ref, w_col0, out_col0, tables, out_scale):
        gc, gs = tables
        t = _mm(h, w_ref[:, w_col0:w_col0 + kv_w])
        for hd in range(n_kv):
            th = t[:, hd * HEAD_DIM:(hd + 1) * HEAD_DIM]
            r = lax.rsqrt(jnp.mean(th * th, axis=-1, keepdims=True) + NORM_EPS)
            if out_scale != 1.0:
                r = r * out_scale
            th = (th * gc + partner(th) * gs) * r
            out_ref[:, out_col0 + hd * HEAD_DIM:out_col0 + (hd + 1) * HEAD_DIM] = th.astype(BF16)

    head_group(k_ref, q_w, 0, gained_tables(kn_ref[...]), 1.0)
    q_tables = gained_tables(qn_ref[...])
    for c0 in range(0, q_w, kv_w):
        head_group(q_ref, c0, c0, q_tables, scale)
    v_ref[...] = _mm(h, w_ref[:, q_w + kv_w:]).astype(BF16)


def _qkv(x2, g, w_qkv, q_norm, k_norm, cos, sin_signed, *, bm, seq, casts):
    m, d = x2.shape
    n_kv = N_KV_HEADS
    kv_w = n_kv * HEAD_DIM
    q_w = w_qkv.shape[1] - 2 * kv_w
    n_heads = q_w // HEAD_DIM
    row = lambda i: (i, 0)
    fixed = lambda i: (0, 0)
    pos = lambda i: (i % (seq // bm), 0)
    est = (2 * bm * d * 4 + w_qkv.size * 2 + 4 * bm * HEAD_DIM * 4
           + 2 * bm * (q_w + 2 * kv_w) * 2 + bm * d * 2 + bm * q_w * 4 * 2)
    return _row_tiled_call(
        functools.partial(_qkv_kernel, n_heads=n_heads, n_kv=n_kv,
                          scale=HEAD_DIM ** -0.5 * math.log2(math.e)),
        name="attn_qkv", n_steps=m // bm,
        in_specs=[
            pl.BlockSpec((bm, d), row),
            pl.BlockSpec((1, d), fixed),
            pl.BlockSpec(w_qkv.shape, fixed),
            pl.BlockSpec((1, HEAD_DIM), fixed),
            pl.BlockSpec((1, HEAD_DIM), fixed),
            pl.BlockSpec((bm, HEAD_DIM), pos),
            pl.BlockSpec((bm, HEAD_DIM), pos),
        ],
        out_specs=[pl.BlockSpec((bm, q_w), row), pl.BlockSpec((bm, kv_w), row),
                   pl.BlockSpec((bm, kv_w), row)],
        out_shape=[jax.ShapeDtypeStruct((m, q_w), BF16),
                   jax.ShapeDtypeStruct((m, kv_w), BF16),
                   jax.ShapeDtypeStruct((m, kv_w), BF16)],
        args=(x2, g, w_qkv, q_norm, k_norm, cos, sin_signed), est=est, casts=casts)


def _attn_kernel(q_ref, k_ref, v_ref, o_ref, *, n_rep, rows):
    qb = q_ref.shape[0]
    for r in range(n_rep):
        cols = slice(r * HEAD_DIM, (r + 1) * HEAD_DIM)
        for r0 in range(0, qb, rows):
            rs = slice(r0, r0 + rows)
            s = lax.dot_general(q_ref[rs, cols], k_ref[...], (((1,), (1,)), ((), ())),
                                preferred_element_type=F32)
            p = jnp.exp2(s - jnp.max(s, axis=-1, keepdims=True))
            l = jnp.sum(p, axis=-1, keepdims=True)
            o_ref[rs, cols] = (_mm(p.astype(BF16), v_ref[...]) / l).astype(BF16)


def _attention(q, k, v, *, batch, seq, qb, rows):
    m, q_w = q.shape
    n_kv = k.shape[1] // HEAD_DIM
    n_rep = q_w // HEAD_DIM // n_kv
    nb = seq // qb
    est = (2 * 2 * qb * n_rep * HEAD_DIM * 2 + 2 * 2 * seq * HEAD_DIM * 2
           + 4 * rows * seq * (4 + 2))
    return pl.pallas_call(
        functools.partial(_attn_kernel, n_rep=n_rep, rows=rows),
        grid=(batch, n_kv, nb),
        in_specs=[
            pl.BlockSpec((qb, n_rep * HEAD_DIM), lambda b, g, i: (b * nb + i, g)),
            pl.BlockSpec((seq, HEAD_DIM), lambda b, g, i: (b, g)),
            pl.BlockSpec((seq, HEAD_DIM), lambda b, g, i: (b, g)),
        ],
        out_specs=pl.BlockSpec((qb, n_rep * HEAD_DIM), lambda b, g, i: (b * nb + i, g)),
        out_shape=jax.ShapeDtypeStruct((m, q_w), BF16),
        compiler_params=pltpu.CompilerParams(
            dimension_semantics=("parallel", "parallel", "parallel"),
            vmem_limit_bytes=_vmem_limit(est)),
        name="attn_core",
    )(q, k, v)


def _proj_kernel(a_ref, w_ref, x_ref, o_ref):
    o_ref[...] = x_ref[...] + _mm(a_ref[...], w_ref[...])


def _proj_residual(a, w, x2, *, bm, casts):
    m, kdim = a.shape
    d = w.shape[1]
    row = lambda i: (i, 0)
    est = 2 * bm * kdim * 2 + w.size * 2 + 2 * 2 * bm * d * 4 + bm * d * 4
    return _row_tiled_call(
        _proj_kernel, name="attn_out", n_steps=m // bm,
        in_specs=[pl.BlockSpec((bm, kdim), row), pl.BlockSpec(w.shape, lambda i: (0, 0)),
                  pl.BlockSpec((bm, d), row)],
        out_specs=[pl.BlockSpec((bm, d), row)],
        out_shape=[jax.ShapeDtypeStruct((m, d), F32)],
        args=(a, w, x2), est=est, casts=casts)


def _rope_tables(seq):
    axis_dim = HEAD_DIM // 2
    t = jnp.arange(seq)
    row = (t // GRID_W).astype(F32)
    col = (t % GRID_W).astype(F32)
    inv_freq = ROPE_THETA ** (-jnp.arange(0, axis_dim, 2, dtype=F32) / axis_dim)
    ang_r = row[:, None] * inv_freq[None, :]
    ang_c = col[:, None] * inv_freq[None, :]
    cr, sr, cc, sc = jnp.cos(ang_r), jnp.sin(ang_r), jnp.cos(ang_c), jnp.sin(ang_c)
    assert cr.shape == (seq, QUARTER)
    return (jnp.concatenate([cr, cc, cr, cc], axis=-1),
            jnp.concatenate([-sr, -sc, sr, sc], axis=-1))


def kernel(x, gm_w_in, gm_ln_g, gm_ln_b, gm_w_s, gm_b_s, gm_w_out, attn_w_qkv, attn_q_norm, attn_k_norm, attn_w_o, ffn_w1, ffn_w2, norm_mix, norm_ffn, norm_final):
    batch, seq, d = x.shape
    assert norm_mix.shape[0] == 2, "two layers: gMLP mixer then attention mixer"
    m = batch * seq
    width = gm_w_out.shape[1]
    x2 = x.reshape(m, d)
    cos, sin_signed = _rope_tables(seq)
    bm = 512
    fc = 1024

    (u, v), (w_out, w1_0) = _gmlp_in(
        x2, norm_mix[0].reshape(1, d), gm_w_in[0].astype(BF16),
        gm_ln_g[0].reshape(1, width), gm_ln_b[0].reshape(1, width), bm=bm,
        casts=(_Cast(gm_w_out, 0), _Cast(ffn_w1, 0)))
    b_full = jnp.repeat(gm_b_s[0].T, width // gm_b_s.shape[1], axis=1)
    (x2,), (w2_0, w_qkv) = _gmlp_out(
        u, v, gm_w_s[0].astype(BF16), b_full, w_out, x2, bm=bm,
        casts=(_Cast(ffn_w2, 0), _Cast(attn_w_qkv, 0, functools.partial(
            _cast_qkv_weights, qk_width=attn_w_qkv.shape[2] - N_KV_HEADS * HEAD_DIM))))
    x2 = _ffn(x2, norm_ffn[0].reshape(1, d), w1_0, w2_0, norm_final.reshape(1, d),
              bm=bm, fc=fc, final_norm=False)

    (q, k, v), (w_o, w1_1) = _qkv(
        x2, norm_mix[1].reshape(1, d), w_qkv, attn_q_norm[0].reshape(1, HEAD_DIM),
        attn_k_norm[0].reshape(1, HEAD_DIM), cos, sin_signed, bm=bm, seq=seq,
        casts=(_Cast(attn_w_o, 0), _Cast(ffn_w1, 1)))
    o = _attention(q, k, v, batch=batch, seq=seq, qb=1024, rows=512)
    (x2,), (w2_1,) = _proj_residual(o, w_o, x2, bm=bm, casts=(_Cast(ffn_w2, 1),))
    x2 = _ffn(x2, norm_ffn[1].reshape(1, d), w1_1, w2_1, norm_final.reshape(1, d),
              bm=bm, fc=fc, final_norm=True)
    return x2.reshape(batch, seq, d)
```

```python
import functools
import math
from typing import Callable, NamedTuple, Optional

import jax
import jax.numpy as jnp
import numpy as np
from jax import lax
from jax.experimental import pallas as pl
from jax.experimental.pallas import tpu as pltpu

NORM_EPS = 1e-6
CHUNK = 128
HEAD_DIM = 128
N_KV_HEADS = 4
GRID_W = 64
ROPE_THETA = 10000.0

V7X_VMEM_BYTES = 64 * 1024 * 1024
VMEM_RESERVE_BYTES = 6 * 1024 * 1024
VMEM_TEMP_BYTES = 8 * 1024 * 1024
LANES = 128

BF16 = jnp.bfloat16
F32 = jnp.float32


def _vmem_limit(estimate_bytes):
    return int(min(V7X_VMEM_BYTES - VMEM_RESERVE_BYTES, estimate_bytes + VMEM_TEMP_BYTES))


def _rmsnorm(xf, g):
    ms = jnp.mean(xf * xf, axis=-1, keepdims=True)
    return xf * lax.rsqrt(ms + NORM_EPS) * g


def _gelu(z):
    return 0.5 * z * (1.0 + lax.erf(z * math.sqrt(0.5)))


def _mm(a, b):
    return jnp.dot(a, b, preferred_element_type=F32)


class _Cast(NamedTuple):
    w: jax.Array
    layer: int
    transform: Optional[Callable] = None


def _call_with_casts(body, *, name, grid, in_specs, out_specs, out_shape, args,
                     est, casts=(), scratch_shapes=()):
    n_in, n_out, n_cast = len(in_specs), len(out_specs), len(casts)
    n_steps = math.prod(grid)

    def step(*idx):
        lin = idx[0]
        for i, n in zip(idx[1:], grid[1:]):
            lin = lin * n + i
        return lin

    cast_layers = [c.layer for c in casts]
    cast_shapes = [c.w.shape[1:] for c in casts]
    cast_fns = [c.transform for c in casts]
    casts = [c.w.reshape(-1, c.w.shape[2]) for c in casts]

    def kernel(*refs):
        ins, refs = refs[:n_in], refs[n_in:]
        cast_in, refs = refs[:n_cast], refs[n_cast:]
        outs, refs = refs[:n_out], refs[n_out:]
        cast_out, scratch = refs[:n_cast], refs[n_cast:]
        body(*ins, *outs, *scratch)
        for src, dst, fn in zip(cast_in, cast_out, cast_fns):
            if fn is None:
                dst[...] = src[...].astype(BF16)
            else:
                fn(src, dst)

    cast_in_specs = [
        pl.BlockSpec((r // n_steps, c),
                     functools.partial(lambda *idx, layer: (layer * n_steps + step(*idx), 0),
                                       layer=layer))
        for (r, c), layer in zip(cast_shapes, cast_layers)]
    cast_out_specs = [pl.BlockSpec((r // n_steps, c), lambda *idx: (step(*idx), 0))
                      for r, c in cast_shapes]
    est += sum(2 * (r // n_steps) * c * (4 + 2) for r, c in cast_shapes)
    outs = pl.pallas_call(
        kernel,
        grid=grid,
        in_specs=list(in_specs) + cast_in_specs,
        out_specs=list(out_specs) + cast_out_specs,
        out_shape=list(out_shape) + [jax.ShapeDtypeStruct(s, BF16) for s in cast_shapes],
        scratch_shapes=list(scratch_shapes),
        compiler_params=pltpu.CompilerParams(
            dimension_semantics=("parallel",) * len(grid), vmem_limit_bytes=_vmem_limit(est)),
        name=name,
    )(*args, *casts)
    return outs[:n_out], outs[n_out:]


def _gmlp_in_kernel(x_ref, g_ref, w_ref, lg_ref, lb_ref, u_ref, v_ref, *, width):
    h = _rmsnorm(x_ref[...], g_ref[...]).astype(BF16)
    v = _gelu(_mm(h, w_ref[:, width:]))
    mu = jnp.mean(v, axis=-1, keepdims=True)
    vc = v - mu
    var = jnp.mean(vc * vc, axis=-1, keepdims=True)
    vn = vc * lax.rsqrt(var + NORM_EPS) * lg_ref[...] + lb_ref[...]
    v_ref[...] = vn.astype(BF16)
    u_ref[...] = _gelu(_mm(h, w_ref[:, :width])).astype(BF16)


def _gmlp_in(x2, g, w_in, ln_g, ln_b, *, bm, casts):
    m, d = x2.shape
    width = w_in.shape[1] // 2
    row = lambda i: (i, 0)
    fixed = lambda i: (0, 0)
    est = (2 * bm * d * 4 + w_in.size * 2 + 2 * 2 * bm * width * 2
           + bm * d * 2 + 3 * bm * width * 4)
    return _call_with_casts(
        functools.partial(_gmlp_in_kernel, width=width),
        name="gmlp_in", grid=(m // bm,),
        in_specs=[
            pl.BlockSpec((bm, d), row),
            pl.BlockSpec((1, d), fixed),
            pl.BlockSpec(w_in.shape, fixed),
            pl.BlockSpec((1, width), fixed),
            pl.BlockSpec((1, width), fixed),
        ],
        out_specs=[pl.BlockSpec((bm, width), row), pl.BlockSpec((bm, width), row)],
        out_shape=[jax.ShapeDtypeStruct((m, width), BF16)] * 2,
        args=(x2, g, w_in, ln_g, ln_b), est=est, casts=casts)


def _gmlp_out_kernel(u_ref, v_ref, ws_ref, bs_ref, wout_ref, x_ref, o_ref, y_ref, *, groups):
    bm = u_ref.shape[0]
    n_chunks = bm // CHUNK
    for g in range(groups):
        cols = slice(g * LANES, (g + 1) * LANES)
        rhs = jnp.concatenate(
            [v_ref[c * CHUNK:(c + 1) * CHUNK, cols] for c in range(n_chunks)], axis=1)
        sg = _mm(ws_ref[g], rhs)
        bias = bs_ref[:, cols]
        for c in range(n_chunks):
            rows = slice(c * CHUNK, (c + 1) * CHUNK)
            s = sg[:, c * LANES:(c + 1) * LANES] + bias
            y_ref[rows, cols] = (u_ref[rows, cols].astype(F32) * s).astype(BF16)
    o_ref[...] = x_ref[...] + _mm(y_ref[...], wout_ref[...])


def _gmlp_out(u, v, w_s, b_full, w_out, x2, *, bm, casts):
    m, width = u.shape
    d = w_out.shape[1]
    groups = w_s.shape[0]
    row = lambda i: (i, 0)
    est = (2 * 2 * bm * width * 2 + w_s.size * 2 + b_full.size * 4 + w_out.size * 2
           + 2 * 2 * bm * d * 4 + bm * width * 2 + bm * d * 4)
    return _call_with_casts(
        functools.partial(_gmlp_out_kernel, groups=groups),
        name="gmlp_out", grid=(m // bm,),
        in_specs=[
            pl.BlockSpec((bm, width), row),
            pl.BlockSpec((bm, width), row),
            pl.BlockSpec(w_s.shape, lambda i: (0, 0, 0)),
            pl.BlockSpec(b_full.shape, lambda i: (0, 0)),
            pl.BlockSpec(w_out.shape, lambda i: (0, 0)),
            pl.BlockSpec((bm, d), row),
        ],
        out_specs=[pl.BlockSpec((bm, d), row)],
        out_shape=[jax.ShapeDtypeStruct((m, d), F32)],
        scratch_shapes=[pltpu.VMEM((bm, width), BF16)],
        args=(u, v, w_s, b_full, w_out, x2), est=est, casts=casts)


def _ffn_kernel(x_ref, g_ref, w1_ref, w2_ref, gf_ref, o_ref, h_ref, *, final_norm):
    j = pl.program_id(1)

    def chunk(h, acc):
        a = jnp.maximum(_mm(h, w1_ref[...]), 0.0)
        o_ref[...] = acc + _mm((a * a).astype(BF16), w2_ref[...])

    @pl.when(j == 0)
    def _():
        xf = x_ref[...]
        h_ref[...] = _rmsnorm(xf, g_ref[...]).astype(BF16)
        chunk(h_ref[...], xf)

    @pl.when(j > 0)
    def _():
        chunk(h_ref[...], o_ref[...])

    if final_norm:
        @pl.when(j == pl.num_programs(1) - 1)
        def _():
            o_ref[...] = _rmsnorm(o_ref[...], gf_ref[...])


def _ffn(x2, g, w1, w2, g_final, *, bm, fc, final_norm):
    m, d = x2.shape
    d_ff = w1.shape[1]
    est = (2 * bm * d * 4 + 2 * d * fc * 2 + 2 * fc * d * 2 + 2 * bm * d * 4
           + bm * d * 2 + bm * fc * 4 + bm * fc * 2 + bm * d * 4)
    return pl.pallas_call(
        functools.partial(_ffn_kernel, final_norm=final_norm),
        grid=(m // bm, d_ff // fc),
        in_specs=[
            pl.BlockSpec((bm, d), lambda i, j: (i, 0)),
            pl.BlockSpec((1, d), lambda i, j: (0, 0)),
            pl.BlockSpec((d, fc), lambda i, j: (0, j)),
            pl.BlockSpec((fc, d), lambda i, j: (j, 0)),
            pl.BlockSpec((1, d), lambda i, j: (0, 0)),
        ],
        out_specs=pl.BlockSpec((bm, d), lambda i, j: (i, 0)),
        out_shape=jax.ShapeDtypeStruct((m, d), F32),
        scratch_shapes=[pltpu.VMEM((bm, d), BF16)],
        compiler_params=pltpu.CompilerParams(
            dimension_semantics=("parallel", "arbitrary"),
            vmem_limit_bytes=_vmem_limit(est)),
        name="ffn_final" if final_norm else "ffn",
    )(x2, g, w1, w2, g_final)


QUARTER = HEAD_DIM // 4


def _swap_mid_quarters(t):
    lane = lax.broadcasted_iota(jnp.int32, t.shape, 1)
    from_right = (lane >= QUARTER) & (lane < 2 * QUARTER)
    from_left = (lane >= 2 * QUARTER) & (lane < 3 * QUARTER)
    return jnp.where(from_right, pltpu.roll(t, HEAD_DIM - QUARTER, axis=1),
                     jnp.where(from_left, pltpu.roll(t, QUARTER, axis=1), t))


def _cast_qkv_weights(src_ref, dst_ref, *, qk_width):
    for c0 in range(0, qk_width, HEAD_DIM):
        cols = slice(c0, c0 + HEAD_DIM)
        dst_ref[:, cols] = _swap_mid_quarters(src_ref[:, cols]).astype(BF16)
    dst_ref[:, qk_width:] = src_ref[:, qk_width:].astype(BF16)


def _qkv_kernel(x_ref, g_ref, w_ref, qn_ref, kn_ref, cos_ref, sin_ref,
                q_ref, k_ref, v_ref, *, n_heads, n_kv, scale):
    h = _rmsnorm(x_ref[...], g_ref[...]).astype(BF16)
    cos = cos_ref[...]
    sin_signed = sin_ref[...]
    q_w = n_heads * HEAD_DIM
    kv_w = n_kv * HEAD_DIM

    def partner(t):
        return pltpu.roll(t, HEAD_DIM // 2, axis=1)

    def gained_tables(gain):
        gain_b = _swap_mid_quarters(jnp.broadcast_to(gain, cos.shape))
        return gain_b * cos, partner(gain_b) * sin_signed

    def head_group(out_ref, w_col0, out_col0, tables, out_scale):
        gc, gs = tables
        t = _mm(h, w_ref[:, w_col0:w_col0 + kv_w])
        for hd in range(n_kv):
            th = t[:, hd * HEAD_DIM:(hd + 1) * HEAD_DIM]
            r = lax.rsqrt(jnp.mean(th * th, axis=-1, keepdims=True) + NORM_EPS)
            if out_scale != 1.0:
                r = r * out_scale
            th = (th * gc + partner(th) * gs) * r
            out_ref[:, out_col0 + hd * HEAD_DIM:out_col0 + (hd + 1) * HEAD_DIM] = th.astype(BF16)

    head_group(k_ref, q_w, 0, gained_tables(kn_ref[...]), 1.0)
    q_tables = gained_tables(qn_ref[...])
    for c0 in range(0, q_w, kv_w):
        head_group(q_ref, c0, c0, q_tables, scale)
    v_ref[...] = _mm(h, w_ref[:, q_w + kv_w:]).astype(BF16)


def _qkv(x2, g, w_qkv, q_norm, k_norm, cos, sin_signed, *, bm, seq, casts):
    m, d = x2.shape
    n_kv = N_KV_HEADS
    kv_w = n_kv * HEAD_DIM
    q_w = w_qkv.shape[1] - 2 * kv_w
    n_heads = q_w // HEAD_DIM
    row = lambda i: (i, 0)
    fixed = lambda i: (0, 0)
    pos = lambda i: (i % (seq // bm), 0)
    est = (2 * bm * d * 4 + w_qkv.size * 2 + 4 * bm * HEAD_DIM * 4
           + 2 * bm * (q_w + 2 * kv_w) * 2 + bm * d * 2 + bm * q_w * 4 * 2)
    return _call_with_casts(
        functools.partial(_qkv_kernel, n_heads=n_heads, n_kv=n_kv,
                          scale=HEAD_DIM ** -0.5 * math.log2(math.e)),
        name="attn_qkv", grid=(m // bm,),
        in_specs=[
            pl.BlockSpec((bm, d), row),
            pl.BlockSpec((1, d), fixed),
            pl.BlockSpec(w_qkv.shape, fixed),
            pl.BlockSpec((1, HEAD_DIM), fixed),
            pl.BlockSpec((1, HEAD_DIM), fixed),
            pl.BlockSpec((bm, HEAD_DIM), pos),
            pl.BlockSpec((bm, HEAD_DIM), pos),
        ],
        out_specs=[pl.BlockSpec((bm, q_w), row), pl.BlockSpec((bm, kv_w), row),
                   pl.BlockSpec((bm, kv_w), row)],
        out_shape=[jax.ShapeDtypeStruct((m, q_w), BF16),
                   jax.ShapeDtypeStruct((m, kv_w), BF16),
                   jax.ShapeDtypeStruct((m, kv_w), BF16)],
        args=(x2, g, w_qkv, q_norm, k_norm, cos, sin_signed), est=est, casts=casts)


def _attn_kernel(q_ref, k_ref, v_ref, o_ref, *, n_rep, rows):
    qb = q_ref.shape[0]
    for r in range(n_rep):
        cols = slice(r * HEAD_DIM, (r + 1) * HEAD_DIM)
        for r0 in range(0, qb, rows):
            rs = slice(r0, r0 + rows)
            s = lax.dot_general(q_ref[rs, cols], k_ref[...], (((1,), (1,)), ((), ())),
                                preferred_element_type=F32)
            p = jnp.exp2(s - jnp.max(s, axis=-1, keepdims=True))
            l = jnp.sum(p, axis=-1, keepdims=True)
            o_ref[rs, cols] = (_mm(p.astype(BF16), v_ref[...]) / l).astype(BF16)


def _attention(q, k, v, *, batch, seq, qb, rows, casts):
    m, q_w = q.shape
    n_kv = k.shape[1] // HEAD_DIM
    n_rep = q_w // HEAD_DIM // n_kv
    nb = seq // qb
    est = (2 * 2 * qb * n_rep * HEAD_DIM * 2 + 2 * 2 * seq * HEAD_DIM * 2
           + 4 * rows * seq * (4 + 2))
    return _call_with_casts(
        functools.partial(_attn_kernel, n_rep=n_rep, rows=rows),
        name="attn_core", grid=(batch, n_kv, nb),
        in_specs=[
            pl.BlockSpec((qb, n_rep * HEAD_DIM), lambda b, g, i: (b * nb + i, g)),
            pl.BlockSpec((seq, HEAD_DIM), lambda b, g, i: (b, g)),
            pl.BlockSpec((seq, HEAD_DIM), lambda b, g, i: (b, g)),
        ],
        out_specs=[pl.BlockSpec((qb, n_rep * HEAD_DIM), lambda b, g, i: (b * nb + i, g))],
        out_shape=[jax.ShapeDtypeStruct((m, q_w), BF16)],
        args=(q, k, v), est=est, casts=casts)


def _proj_kernel(a_ref, w_ref, x_ref, o_ref):
    o_ref[...] = x_ref[...] + _mm(a_ref[...], w_ref[...])


def _proj_residual(a, w, x2, *, bm, casts):
    m, kdim = a.shape
    d = w.shape[1]
    row = lambda i: (i, 0)
    est = 2 * bm * kdim * 2 + w.size * 2 + 2 * 2 * bm * d * 4 + bm * d * 4
    return _call_with_casts(
        _proj_kernel, name="attn_out", grid=(m // bm,),
        in_specs=[pl.BlockSpec((bm, kdim), row), pl.BlockSpec(w.shape, lambda i: (0, 0)),
                  pl.BlockSpec((bm, d), row)],
        out_specs=[pl.BlockSpec((bm, d), row)],
        out_shape=[jax.ShapeDtypeStruct((m, d), F32)],
        args=(a, w, x2), est=est, casts=casts)


def _rope_tables(seq):
    axis_dim = HEAD_DIM // 2
    t = np.arange(seq)
    inv_freq = ROPE_THETA ** (-np.arange(0, axis_dim, 2, dtype=np.float64) / axis_dim)
    ang_r = (t // GRID_W)[:, None] * inv_freq[None, :]
    ang_c = (t % GRID_W)[:, None] * inv_freq[None, :]
    cr, sr, cc, sc = np.cos(ang_r), np.sin(ang_r), np.cos(ang_c), np.sin(ang_c)
    assert cr.shape == (seq, QUARTER)
    return (jnp.asarray(np.concatenate([cr, cc, cr, cc], axis=-1), dtype=F32),
            jnp.asarray(np.concatenate([-sr, -sc, sr, sc], axis=-1), dtype=F32))


def kernel(x, gm_w_in, gm_ln_g, gm_ln_b, gm_w_s, gm_b_s, gm_w_out, attn_w_qkv, attn_q_norm, attn_k_norm, attn_w_o, ffn_w1, ffn_w2, norm_mix, norm_ffn, norm_final):
    batch, seq, d = x.shape
    assert norm_mix.shape[0] == 2, "two layers: gMLP mixer then attention mixer"
    m = batch * seq
    width = gm_w_out.shape[1]
    x2 = x.reshape(m, d)
    cos, sin_signed = _rope_tables(seq)
    bm = 512
    fc = 2048

    (u, v), (w_out, w1_0, w2_0) = _gmlp_in(
        x2, norm_mix[0].reshape(1, d), gm_w_in[0].astype(BF16),
        gm_ln_g[0].reshape(1, width), gm_ln_b[0].reshape(1, width), bm=bm,
        casts=(_Cast(gm_w_out, 0), _Cast(ffn_w1, 0), _Cast(ffn_w2, 0)))
    b_full = jnp.repeat(gm_b_s[0].T, width // gm_b_s.shape[1], axis=1)
    (x2,), (w_qkv,) = _gmlp_out(
        u, v, gm_w_s[0].astype(BF16), b_full, w_out, x2, bm=bm,
        casts=(_Cast(attn_w_qkv, 0, functools.partial(
            _cast_qkv_weights, qk_width=attn_w_qkv.shape[2] - N_KV_HEADS * HEAD_DIM)),))
    x2 = _ffn(x2, norm_ffn[0].reshape(1, d), w1_0, w2_0, norm_final.reshape(1, d),
              bm=bm, fc=fc, final_norm=False)

    (q, k, v), (w_o, w1_1) = _qkv(
        x2, norm_mix[1].reshape(1, d), w_qkv, attn_q_norm[0].reshape(1, HEAD_DIM),
        attn_k_norm[0].reshape(1, HEAD_DIM), cos, sin_signed, bm=bm, seq=seq,
        casts=(_Cast(attn_w_o, 0), _Cast(ffn_w1, 1)))
    (o,), (w2_1,) = _attention(q, k, v, batch=batch, seq=seq, qb=1024, rows=512,
                               casts=(_Cast(ffn_w2, 1),))
    (x2,), _ = _proj_residual(o, w_o, x2, bm=bm, casts=())
    x2 = _ffn(x2, norm_ffn[1].reshape(1, d), w1_1, w2_1, norm_final.reshape(1, d),
              bm=bm, fc=fc, final_norm=True)
    return x2.reshape(batch, seq, d)
```

```python
import functools
import math
from typing import Callable, NamedTuple, Optional

import jax
import jax.numpy as jnp
import numpy as np
from jax import lax
from jax.experimental import pallas as pl
from jax.experimental.pallas import tpu as pltpu

NORM_EPS = 1e-6
CHUNK = 128
HEAD_DIM = 128
N_KV_HEADS = 4
GRID_W = 64
ROPE_THETA = 10000.0

V7X_VMEM_BYTES = 64 * 1024 * 1024
VMEM_RESERVE_BYTES = 6 * 1024 * 1024
VMEM_TEMP_BYTES = 8 * 1024 * 1024
LANES = 128

BF16 = jnp.bfloat16
F32 = jnp.float32


def _vmem_limit(estimate_bytes):
    return int(min(V7X_VMEM_BYTES - VMEM_RESERVE_BYTES, estimate_bytes + VMEM_TEMP_BYTES))


def _rmsnorm(xf, g):
    ms = jnp.mean(xf * xf, axis=-1, keepdims=True)
    return xf * lax.rsqrt(ms + NORM_EPS) * g


def _gelu(z):
    return 0.5 * z * (1.0 + lax.erf(z * math.sqrt(0.5)))


def _mm(a, b):
    return jnp.dot(a, b, preferred_element_type=F32)


class _Cast(NamedTuple):
    w: jax.Array
    layer: int
    transform: Optional[Callable] = None


def _call_with_casts(body, *, name, grid, in_specs, out_specs, out_shape, args,
                     est, casts=(), scratch_shapes=()):
    n_in, n_out, n_cast = len(in_specs), len(out_specs), len(casts)
    n_steps = math.prod(grid)

    def step(*idx):
        lin = idx[0]
        for i, n in zip(idx[1:], grid[1:]):
            lin = lin * n + i
        return lin

    cast_layers = [c.layer for c in casts]
    cast_shapes = [c.w.shape[1:] for c in casts]
    cast_fns = [c.transform for c in casts]
    casts = [c.w.reshape(-1, c.w.shape[2]) for c in casts]

    def kernel(*refs):
        ins, refs = refs[:n_in], refs[n_in:]
        cast_in, refs = refs[:n_cast], refs[n_cast:]
        outs, refs = refs[:n_out], refs[n_out:]
        cast_out, scratch = refs[:n_cast], refs[n_cast:]
        body(*ins, *outs, *scratch)
        for src, dst, fn in zip(cast_in, cast_out, cast_fns):
            if fn is None:
                dst[...] = src[...].astype(BF16)
            else:
                fn(src, dst)

    cast_in_specs = [
        pl.BlockSpec((r // n_steps, c),
                     functools.partial(lambda *idx, layer: (layer * n_steps + step(*idx), 0),
                                       layer=layer))
        for (r, c), layer in zip(cast_shapes, cast_layers)]
    cast_out_specs = [pl.BlockSpec((r // n_steps, c), lambda *idx: (step(*idx), 0))
                      for r, c in cast_shapes]
    est += sum(2 * (r // n_steps) * c * (4 + 2) for r, c in cast_shapes)
    outs = pl.pallas_call(
        kernel,
        grid=grid,
        in_specs=list(in_specs) + cast_in_specs,
        out_specs=list(out_specs) + cast_out_specs,
        out_shape=list(out_shape) + [jax.ShapeDtypeStruct(s, BF16) for s in cast_shapes],
        scratch_shapes=list(scratch_shapes),
        compiler_params=pltpu.CompilerParams(
            dimension_semantics=("parallel",) * len(grid), vmem_limit_bytes=_vmem_limit(est)),
        name=name,
    )(*args, *casts)
    return outs[:n_out], outs[n_out:]


def _gmlp_in_kernel(x_ref, g_ref, w_ref, lg_ref, lb_ref, u_ref, v_ref, *, width):
    h = _rmsnorm(x_ref[...], g_ref[...]).astype(BF16)
    v = _gelu(_mm(h, w_ref[:, width:]))
    mu = jnp.mean(v, axis=-1, keepdims=True)
    vc = v - mu
    var = jnp.mean(vc * vc, axis=-1, keepdims=True)
    vn = vc * lax.rsqrt(var + NORM_EPS) * lg_ref[...] + lb_ref[...]
    v_ref[...] = vn.astype(BF16)
    u_ref[...] = _gelu(_mm(h, w_ref[:, :width])).astype(BF16)


def _gmlp_in(x2, g, w_in, ln_g, ln_b, *, bm, casts):
    m, d = x2.shape
    width = w_in.shape[1] // 2
    row = lambda i: (i, 0)
    fixed = lambda i: (0, 0)
    est = (2 * bm * d * 4 + w_in.size * 2 + 2 * 2 * bm * width * 2
           + bm * d * 2 + 3 * bm * width * 4)
    return _call_with_casts(
        functools.partial(_gmlp_in_kernel, width=width),
        name="gmlp_in", grid=(m // bm,),
        in_specs=[
            pl.BlockSpec((bm, d), row),
            pl.BlockSpec((1, d), fixed),
            pl.BlockSpec(w_in.shape, fixed),
            pl.BlockSpec((1, width), fixed),
            pl.BlockSpec((1, width), fixed),
        ],
        out_specs=[pl.BlockSpec((bm, width), row), pl.BlockSpec((bm, width), row)],
        out_shape=[jax.ShapeDtypeStruct((m, width), BF16)] * 2,
        args=(x2, g, w_in, ln_g, ln_b), est=est, casts=casts)


def _gmlp_out_kernel(u_ref, v_ref, ws_ref, bs_ref, wout_ref, x_ref, o_ref, y_ref, *, groups):
    bm = u_ref.shape[0]
    n_chunks = bm // CHUNK
    for g in range(groups):
        cols = slice(g * LANES, (g + 1) * LANES)
        rhs = jnp.concatenate(
            [v_ref[c * CHUNK:(c + 1) * CHUNK, cols] for c in range(n_chunks)], axis=1)
        sg = _mm(ws_ref[g], rhs)
        bias = bs_ref[:, cols]
        for c in range(n_chunks):
            rows = slice(c * CHUNK, (c + 1) * CHUNK)
            s = sg[:, c * LANES:(c + 1) * LANES] + bias
            y_ref[rows, cols] = (u_ref[rows, cols].astype(F32) * s).astype(BF16)
    o_ref[...] = x_ref[...] + _mm(y_ref[...], wout_ref[...])


def _gmlp_out(u, v, w_s, b_full, w_out, x2, *, bm, casts):
    m, width = u.shape
    d = w_out.shape[1]
    groups = w_s.shape[0]
    row = lambda i: (i, 0)
    est = (2 * 2 * bm * width * 2 + w_s.size * 2 + b_full.size * 4 + w_out.size * 2
           + 2 * 2 * bm * d * 4 + bm * width * 2 + bm * d * 4)
    return _call_with_casts(
        functools.partial(_gmlp_out_kernel, groups=groups),
        name="gmlp_out", grid=(m // bm,),
        in_specs=[
            pl.BlockSpec((bm, width), row),
            pl.BlockSpec((bm, width), row),
            pl.BlockSpec(w_s.shape, lambda i: (0, 0, 0)),
            pl.BlockSpec(b_full.shape, lambda i: (0, 0)),
            pl.BlockSpec(w_out.shape, lambda i: (0, 0)),
            pl.BlockSpec((bm, d), row),
        ],
        out_specs=[pl.BlockSpec((bm, d), row)],
        out_shape=[jax.ShapeDtypeStruct((m, d), F32)],
        scratch_shapes=[pltpu.VMEM((bm, width), BF16)],
        args=(u, v, w_s, b_full, w_out, x2), est=est, casts=casts)


def _ffn_kernel(x_ref, g_ref, w1_ref, w2_ref, gf_ref, o_ref, h_ref, *, final_norm):
    j = pl.program_id(1)

    def chunk(h, acc):
        a = jnp.maximum(_mm(h, w1_ref[...]), 0.0)
        o_ref[...] = acc + _mm((a * a).astype(BF16), w2_ref[...])

    @pl.when(j == 0)
    def _():
        xf = x_ref[...]
        h_ref[...] = _rmsnorm(xf, g_ref[...]).astype(BF16)
        chunk(h_ref[...], xf)

    @pl.when(j > 0)
    def _():
        chunk(h_ref[...], o_ref[...])

    if final_norm:
        @pl.when(j == pl.num_programs(1) - 1)
        def _():
            o_ref[...] = _rmsnorm(o_ref[...], gf_ref[...])


def _ffn(x2, g, w1, w2, g_final, *, bm, fc, final_norm):
    m, d = x2.shape
    d_ff = w1.shape[1]
    est = (2 * bm * d * 4 + 2 * d * fc * 2 + 2 * fc * d * 2 + 2 * bm * d * 4
           + bm * d * 2 + bm * fc * 4 + bm * fc * 2 + bm * d * 4)
    return pl.pallas_call(
        functools.partial(_ffn_kernel, final_norm=final_norm),
        grid=(m // bm, d_ff // fc),
        in_specs=[
            pl.BlockSpec((bm, d), lambda i, j: (i, 0)),
            pl.BlockSpec((1, d), lambda i, j: (0, 0)),
            pl.BlockSpec((d, fc), lambda i, j: (0, j)),
            pl.BlockSpec((fc, d), lambda i, j: (j, 0)),
            pl.BlockSpec((1, d), lambda i, j: (0, 0)),
        ],
        out_specs=pl.BlockSpec((bm, d), lambda i, j: (i, 0)),
        out_shape=jax.ShapeDtypeStruct((m, d), F32),
        scratch_shapes=[pltpu.VMEM((bm, d), BF16)],
        compiler_params=pltpu.CompilerParams(
            dimension_semantics=("parallel", "arbitrary"),
            vmem_limit_bytes=_vmem_limit(est)),
        name="ffn_final" if final_norm else "ffn",
    )(x2, g, w1, w2, g_final)


QUARTER = HEAD_DIM // 4


def _swap_mid_quarters(t):
    lane = lax.broadcasted_iota(jnp.int32, t.shape, 1)
    from_right = (lane >= QUARTER) & (lane < 2 * QUARTER)
    from_left = (lane >= 2 * QUARTER) & (lane < 3 * QUARTER)
    return jnp.where(from_right, pltpu.roll(t, HEAD_DIM - QUARTER, axis=1),
                     jnp.where(from_left, pltpu.roll(t, QUARTER, axis=1), t))


def _cast_qkv_weights(src_ref, dst_ref, *, qk_width):
    for c0 in range(0, qk_width, HEAD_DIM):
        cols = slice(c0, c0 + HEAD_DIM)
        dst_ref[:, cols] = _swap_mid_quarters(src_ref[:, cols]).astype(BF16)
    dst_ref[:, qk_width:] = src_ref[:, qk_width:].astype(BF16)


def _qkv_kernel(x_ref, g_ref, w_ref, qn_ref, kn_ref, cos_ref, sin_ref,
                q_ref, k_ref, v_ref, *, n_heads, n_kv, scale):
    h = _rmsnorm(x_ref[...], g_ref[...]).astype(BF16)
    cos = cos_ref[...]
    sin_signed = sin_ref[...]
    q_w = n_heads * HEAD_DIM
    kv_w = n_kv * HEAD_DIM

    def partner(t):
        return pltpu.roll(t, HEAD_DIM // 2, axis=1)

    def gained_tables(gain):
        gain_b = _swap_mid_quarters(jnp.broadcast_to(gain, cos.shape))
        return gain_b * cos, partner(gain_b) * sin_signed

    def head_group(out_ref, w_col0, out_col0, tables, out_scale):
        gc, gs = tables
        t = _mm(h, w_ref[:, w_col0:w_col0 + kv_w])
        for hd in range(n_kv):
            th = t[:, hd * HEAD_DIM:(hd + 1) * HEAD_DIM]
            r = lax.rsqrt(jnp.mean(th * th, axis=-1, keepdims=True) + NORM_EPS)
            if out_scale != 1.0:
                r = r * out_scale
            th = (th * gc + partner(th) * gs) * r
            out_ref[:, out_col0 + hd * HEAD_DIM:out_col0 + (hd + 1) * HEAD_DIM] = th.astype(BF16)

    head_group(k_ref, q_w, 0, gained_tables(kn_ref[...]), 1.0)
    q_tables = gained_tables(qn_ref[...])
    for c0 in range(0, q_w, kv_w):
        head_group(q_ref, c0, c0, q_tables, scale)
    v_ref[...] = _mm(h, w_ref[:, q_w + kv_w:]).astype(BF16)


def _qkv(x2, g, w_qkv, q_norm, k_norm, cos, sin_signed, *, bm, seq, casts):
    m, d = x2.shape
    n_kv = N_KV_HEADS
    kv_w = n_kv * HEAD_DIM
    q_w = w_qkv.shape[1] - 2 * kv_w
    n_heads = q_w // HEAD_DIM
    row = lambda i: (i, 0)
    fixed = lambda i: (0, 0)
    pos = lambda i: (i % (seq // bm), 0)
    est = (2 * bm * d * 4 + w_qkv.size * 2 + 4 * bm * HEAD_DIM * 4
           + 2 * bm * (q_w + 2 * kv_w) * 2 + bm * d * 2 + bm * q_w * 4 * 2)
    return _call_with_casts(
        functools.partial(_qkv_kernel, n_heads=n_heads, n_kv=n_kv,
                          scale=HEAD_DIM ** -0.5 * math.log2(math.e)),
        name="attn_qkv", grid=(m // bm,),
        in_specs=[
            pl.BlockSpec((bm, d), row),
            pl.BlockSpec((1, d), fixed),
            pl.BlockSpec(w_qkv.shape, fixed),
            pl.BlockSpec((1, HEAD_DIM), fixed),
            pl.BlockSpec((1, HEAD_DIM), fixed),
            pl.BlockSpec((bm, HEAD_DIM), pos),
            pl.BlockSpec((bm, HEAD_DIM), pos),
        ],
        out_specs=[pl.BlockSpec((bm, q_w), row), pl.BlockSpec((bm, kv_w), row),
                   pl.BlockSpec((bm, kv_w), row)],
        out_shape=[jax.ShapeDtypeStruct((m, q_w), BF16),
                   jax.ShapeDtypeStruct((m, kv_w), BF16),
                   jax.ShapeDtypeStruct((m, kv_w), BF16)],
        args=(x2, g, w_qkv, q_norm, k_norm, cos, sin_signed), est=est, casts=casts)


def _attn_kernel(q_ref, k_ref, v_ref, o_ref, *, n_rep, rows):
    qb = q_ref.shape[0]
    for r in range(n_rep):
        cols = slice(r * HEAD_DIM, (r + 1) * HEAD_DIM)
        for r0 in range(0, qb, rows):
            rs = slice(r0, r0 + rows)
            s = lax.dot_general(q_ref[rs, cols], k_ref[...], (((1,), (1,)), ((), ())),
                                preferred_element_type=F32)
            p = jnp.exp2(s - jnp.max(s, axis=-1, keepdims=True))
            l = jnp.sum(p, axis=-1, keepdims=True)
            o_ref[rs, cols] = (_mm(p.astype(BF16), v_ref[...]) / l).astype(BF16)


def _attention(q, k, v, *, batch, seq, qb, rows, casts):
    m, q_w = q.shape
    n_kv = k.shape[1] // HEAD_DIM
    n_rep = q_w // HEAD_DIM // n_kv
    nb = seq // qb
    n_chains = n_rep * (qb // rows)
    est = (2 * 2 * qb * n_rep * HEAD_DIM * 2 + 2 * 2 * seq * HEAD_DIM * 2
           + max(4, n_chains // 2) * rows * seq * (4 + 2))
    return _call_with_casts(
        functools.partial(_attn_kernel, n_rep=n_rep, rows=rows),
        name="attn_core", grid=(batch, n_kv, nb),
        in_specs=[
            pl.BlockSpec((qb, n_rep * HEAD_DIM), lambda b, g, i: (b * nb + i, g)),
            pl.BlockSpec((seq, HEAD_DIM), lambda b, g, i: (b, g)),
            pl.BlockSpec((seq, HEAD_DIM), lambda b, g, i: (b, g)),
        ],
        out_specs=[pl.BlockSpec((qb, n_rep * HEAD_DIM), lambda b, g, i: (b * nb + i, g))],
        out_shape=[jax.ShapeDtypeStruct((m, q_w), BF16)],
        args=(q, k, v), est=est, casts=casts)


def _proj_kernel(a_ref, w_ref, x_ref, o_ref):
    o_ref[...] = x_ref[...] + _mm(a_ref[...], w_ref[...])


def _proj_residual(a, w, x2, *, bm, casts):
    m, kdim = a.shape
    d = w.shape[1]
    row = lambda i: (i, 0)
    est = 2 * bm * kdim * 2 + w.size * 2 + 2 * 2 * bm * d * 4 + bm * d * 4
    return _call_with_casts(
        _proj_kernel, name="attn_out", grid=(m // bm,),
        in_specs=[pl.BlockSpec((bm, kdim), row), pl.BlockSpec(w.shape, lambda i: (0, 0)),
                  pl.BlockSpec((bm, d), row)],
        out_specs=[pl.BlockSpec((bm, d), row)],
        out_shape=[jax.ShapeDtypeStruct((m, d), F32)],
        args=(a, w, x2), est=est, casts=casts)


def _rope_tables(seq):
    axis_dim = HEAD_DIM // 2
    t = np.arange(seq)
    inv_freq = ROPE_THETA ** (-np.arange(0, axis_dim, 2, dtype=np.float64) / axis_dim)
    ang_r = (t // GRID_W)[:, None] * inv_freq[None, :]
    ang_c = (t % GRID_W)[:, None] * inv_freq[None, :]
    cr, sr, cc, sc = np.cos(ang_r), np.sin(ang_r), np.cos(ang_c), np.sin(ang_c)
    assert cr.shape == (seq, QUARTER)
    return (jnp.asarray(np.concatenate([cr, cc, cr, cc], axis=-1), dtype=F32),
            jnp.asarray(np.concatenate([-sr, -sc, sr, sc], axis=-1), dtype=F32))


def kernel(x, gm_w_in, gm_ln_g, gm_ln_b, gm_w_s, gm_b_s, gm_w_out, attn_w_qkv, attn_q_norm, attn_k_norm, attn_w_o, ffn_w1, ffn_w2, norm_mix, norm_ffn, norm_final):
    batch, seq, d = x.shape
    assert norm_mix.shape[0] == 2, "two layers: gMLP mixer then attention mixer"
    m = batch * seq
    width = gm_w_out.shape[1]
    x2 = x.reshape(m, d)
    cos, sin_signed = _rope_tables(seq)
    bm = 512
    fc = 2048

    (u, v), (w_out, w1_0, w2_0) = _gmlp_in(
        x2, norm_mix[0].reshape(1, d), gm_w_in[0].astype(BF16),
        gm_ln_g[0].reshape(1, width), gm_ln_b[0].reshape(1, width), bm=bm,
        casts=(_Cast(gm_w_out, 0), _Cast(ffn_w1, 0), _Cast(ffn_w2, 0)))
    b_full = jnp.repeat(gm_b_s[0].T, width // gm_b_s.shape[1], axis=1)
    (x2,), (w_qkv,) = _gmlp_out(
        u, v, gm_w_s[0].astype(BF16), b_full, w_out, x2, bm=bm,
        casts=(_Cast(attn_w_qkv, 0, functools.partial(
            _cast_qkv_weights, qk_width=attn_w_qkv.shape[2] - N_KV_HEADS * HEAD_DIM)),))
    x2 = _ffn(x2, norm_ffn[0].reshape(1, d), w1_0, w2_0, norm_final.reshape(1, d),
              bm=bm, fc=fc, final_norm=False)

    (q, k, v), (w_o, w1_1, w2_1) = _qkv(
        x2, norm_mix[1].reshape(1, d), w_qkv, attn_q_norm[0].reshape(1, HEAD_DIM),
        attn_k_norm[0].reshape(1, HEAD_DIM), cos, sin_signed, bm=bm, seq=seq,
        casts=(_Cast(attn_w_o, 0), _Cast(ffn_w1, 1), _Cast(ffn_w2, 1)))
    (o,), _ = _attention(q, k, v, batch=batch, seq=seq, qb=2048, rows=512, casts=())
    (x2,), _ = _proj_residual(o, w_o, x2, bm=bm, casts=())
    x2 = _ffn(x2, norm_ffn[1].reshape(1, d), w1_1, w2_1, norm_final.reshape(1, d),
              bm=bm, fc=fc, final_norm=True)
    return x2.reshape(batch, seq, d)
```

```python
import functools
import math
from typing import Callable, NamedTuple, Optional

import jax
import jax.numpy as jnp
import numpy as np
from jax import lax
from jax.experimental import pallas as pl
from jax.experimental.pallas import tpu as pltpu

NORM_EPS = 1e-6
CHUNK = 128
HEAD_DIM = 128
N_KV_HEADS = 4
GRID_W = 64
ROPE_THETA = 10000.0

V7X_VMEM_BYTES = 64 * 1024 * 1024
VMEM_RESERVE_BYTES = 6 * 1024 * 1024
VMEM_TEMP_BYTES = 8 * 1024 * 1024
LANES = 128

BF16 = jnp.bfloat16
F32 = jnp.float32


def _vmem_limit(estimate_bytes):
    return int(min(V7X_VMEM_BYTES - VMEM_RESERVE_BYTES, estimate_bytes + VMEM_TEMP_BYTES))


def _rmsnorm(xf, g):
    ms = jnp.mean(xf * xf, axis=-1, keepdims=True)
    return xf * lax.rsqrt(ms + NORM_EPS) * g


def _gelu(z):
    return 0.5 * z * (1.0 + lax.erf(z * math.sqrt(0.5)))


def _mm(a, b):
    return jnp.dot(a, b, preferred_element_type=F32)


class _Cast(NamedTuple):
    w: jax.Array
    layer: int
    transform: Optional[Callable] = None


def _call_with_casts(body, *, name, grid, in_specs, out_specs, out_shape, args,
                     est, casts=(), scratch_shapes=()):
    n_in, n_out, n_cast = len(in_specs), len(out_specs), len(casts)
    n_steps = math.prod(grid)

    def step(*idx):
        lin = idx[0]
        for i, n in zip(idx[1:], grid[1:]):
            lin = lin * n + i
        return lin

    cast_layers = [c.layer for c in casts]
    cast_shapes = [c.w.shape[1:] for c in casts]
    cast_fns = [c.transform for c in casts]
    casts = [c.w.reshape(-1, c.w.shape[2]) for c in casts]

    def kernel(*refs):
        ins, refs = refs[:n_in], refs[n_in:]
        cast_in, refs = refs[:n_cast], refs[n_cast:]
        outs, refs = refs[:n_out], refs[n_out:]
        cast_out, scratch = refs[:n_cast], refs[n_cast:]
        body(*ins, *outs, *scratch)
        for src, dst, fn in zip(cast_in, cast_out, cast_fns):
            if fn is None:
                dst[...] = src[...].astype(BF16)
            else:
                fn(src, dst)

    cast_in_specs = [
        pl.BlockSpec((r // n_steps, c),
                     functools.partial(lambda *idx, layer: (layer * n_steps + step(*idx), 0),
                                       layer=layer))
        for (r, c), layer in zip(cast_shapes, cast_layers)]
    cast_out_specs = [pl.BlockSpec((r // n_steps, c), lambda *idx: (step(*idx), 0))
                      for r, c in cast_shapes]
    est += sum(2 * (r // n_steps) * c * (4 + 2) for r, c in cast_shapes)
    outs = pl.pallas_call(
        kernel,
        grid=grid,
        in_specs=list(in_specs) + cast_in_specs,
        out_specs=list(out_specs) + cast_out_specs,
        out_shape=list(out_shape) + [jax.ShapeDtypeStruct(s, BF16) for s in cast_shapes],
        scratch_shapes=list(scratch_shapes),
        compiler_params=pltpu.CompilerParams(
            dimension_semantics=("parallel",) * len(grid), vmem_limit_bytes=_vmem_limit(est)),
        name=name,
    )(*args, *casts)
    return outs[:n_out], outs[n_out:]


def _gmlp_in_kernel(x_ref, g_ref, w_ref, lg_ref, lb_ref, u_ref, v_ref, *, width):
    h = _rmsnorm(x_ref[...], g_ref[...]).astype(BF16)
    v = _gelu(_mm(h, w_ref[:, width:]))
    mu = jnp.mean(v, axis=-1, keepdims=True)
    vc = v - mu
    var = jnp.mean(vc * vc, axis=-1, keepdims=True)
    vn = vc * lax.rsqrt(var + NORM_EPS) * lg_ref[...] + lb_ref[...]
    v_ref[...] = vn.astype(BF16)
    u_ref[...] = _gelu(_mm(h, w_ref[:, :width])).astype(BF16)


def _gmlp_in(x2, g, w_in, ln_g, ln_b, *, bm, casts):
    m, d = x2.shape
    width = w_in.shape[1] // 2
    row = lambda i: (i, 0)
    fixed = lambda i: (0, 0)
    est = (2 * bm * d * 4 + w_in.size * 2 + 2 * 2 * bm * width * 2
           + bm * d * 2 + 3 * bm * width * 4)
    return _call_with_casts(
        functools.partial(_gmlp_in_kernel, width=width),
        name="gmlp_in", grid=(m // bm,),
        in_specs=[
            pl.BlockSpec((bm, d), row),
            pl.BlockSpec((1, d), fixed),
            pl.BlockSpec(w_in.shape, fixed),
            pl.BlockSpec((1, width), fixed),
            pl.BlockSpec((1, width), fixed),
        ],
        out_specs=[pl.BlockSpec((bm, width), row), pl.BlockSpec((bm, width), row)],
        out_shape=[jax.ShapeDtypeStruct((m, width), BF16)] * 2,
        args=(x2, g, w_in, ln_g, ln_b), est=est, casts=casts)


def _gmlp_out_kernel(u_ref, v_ref, ws_ref, bs_ref, wout_ref, x_ref, o_ref, y_ref, *, groups):
    bm = u_ref.shape[0]
    n_chunks = bm // CHUNK
    for g in range(groups):
        cols = slice(g * LANES, (g + 1) * LANES)
        rhs = jnp.concatenate(
            [v_ref[c * CHUNK:(c + 1) * CHUNK, cols] for c in range(n_chunks)], axis=1)
        sg = _mm(ws_ref[g], rhs)
        bias = bs_ref[:, cols]
        for c in range(n_chunks):
            rows = slice(c * CHUNK, (c + 1) * CHUNK)
            s = sg[:, c * LANES:(c + 1) * LANES] + bias
            y_ref[rows, cols] = (u_ref[rows, cols].astype(F32) * s).astype(BF16)
    o_ref[...] = x_ref[...] + _mm(y_ref[...], wout_ref[...])


def _gmlp_out(u, v, w_s, b_full, w_out, x2, *, bm, casts):
    m, width = u.shape
    d = w_out.shape[1]
    groups = w_s.shape[0]
    row = lambda i: (i, 0)
    est = (2 * 2 * bm * width * 2 + w_s.size * 2 + b_full.size * 4 + w_out.size * 2
           + 2 * 2 * bm * d * 4 + bm * width * 2 + bm * d * 4)
    return _call_with_casts(
        functools.partial(_gmlp_out_kernel, groups=groups),
        name="gmlp_out", grid=(m // bm,),
        in_specs=[
            pl.BlockSpec((bm, width), row),
            pl.BlockSpec((bm, width), row),
            pl.BlockSpec(w_s.shape, lambda i: (0, 0, 0)),
            pl.BlockSpec(b_full.shape, lambda i: (0, 0)),
            pl.BlockSpec(w_out.shape, lambda i: (0, 0)),
            pl.BlockSpec((bm, d), row),
        ],
        out_specs=[pl.BlockSpec((bm, d), row)],
        out_shape=[jax.ShapeDtypeStruct((m, d), F32)],
        scratch_shapes=[pltpu.VMEM((bm, width), BF16)],
        args=(u, v, w_s, b_full, w_out, x2), est=est, casts=casts)


def _ffn_kernel(x_ref, g_ref, w1_ref, w2_ref, gf_ref, o_ref, h_ref, *, final_norm):
    j = pl.program_id(1)

    def chunk(h, acc):
        a = jnp.maximum(_mm(h, w1_ref[...]), 0.0)
        o_ref[...] = acc + _mm((a * a).astype(BF16), w2_ref[...])

    @pl.when(j == 0)
    def _():
        xf = x_ref[...]
        h_ref[...] = _rmsnorm(xf, g_ref[...]).astype(BF16)
        chunk(h_ref[...], xf)

    @pl.when(j > 0)
    def _():
        chunk(h_ref[...], o_ref[...])

    if final_norm:
        @pl.when(j == pl.num_programs(1) - 1)
        def _():
            o_ref[...] = _rmsnorm(o_ref[...], gf_ref[...])


def _ffn(x2, g, w1, w2, g_final, *, bm, fc, final_norm):
    m, d = x2.shape
    d_ff = w1.shape[1]
    est = (2 * bm * d * 4 + 2 * d * fc * 2 + 2 * fc * d * 2 + 2 * bm * d * 4
           + bm * d * 2 + bm * fc * 4 + bm * fc * 2 + bm * d * 4)
    return pl.pallas_call(
        functools.partial(_ffn_kernel, final_norm=final_norm),
        grid=(m // bm, d_ff // fc),
        in_specs=[
            pl.BlockSpec((bm, d), lambda i, j: (i, 0)),
            pl.BlockSpec((1, d), lambda i, j: (0, 0)),
            pl.BlockSpec((d, fc), lambda i, j: (0, j)),
            pl.BlockSpec((fc, d), lambda i, j: (j, 0)),
            pl.BlockSpec((1, d), lambda i, j: (0, 0)),
        ],
        out_specs=pl.BlockSpec((bm, d), lambda i, j: (i, 0)),
        out_shape=jax.ShapeDtypeStruct((m, d), F32),
        scratch_shapes=[pltpu.VMEM((bm, d), BF16)],
        compiler_params=pltpu.CompilerParams(
            dimension_semantics=("parallel", "arbitrary"),
            vmem_limit_bytes=_vmem_limit(est)),
        name="ffn_final" if final_norm else "ffn",
    )(x2, g, w1, w2, g_final)


QUARTER = HEAD_DIM // 4


def _swap_mid_quarters(t):
    lane = lax.broadcasted_iota(jnp.int32, t.shape, 1)
    from_right = (lane >= QUARTER) & (lane < 2 * QUARTER)
    from_left = (lane >= 2 * QUARTER) & (lane < 3 * QUARTER)
    return jnp.where(from_right, pltpu.roll(t, HEAD_DIM - QUARTER, axis=1),
                     jnp.where(from_left, pltpu.roll(t, QUARTER, axis=1), t))


def _cast_qkv_weights(src_ref, dst_ref, *, qk_width):
    for c0 in range(0, qk_width, HEAD_DIM):
        cols = slice(c0, c0 + HEAD_DIM)
        dst_ref[:, cols] = _swap_mid_quarters(src_ref[:, cols]).astype(BF16)
    dst_ref[:, qk_width:] = src_ref[:, qk_width:].astype(BF16)


def _qkv_kernel(x_ref, g_ref, w_ref, qn_ref, kn_ref, cos_ref, sin_ref,
                q_ref, k_ref, v_ref, *, n_heads, n_kv, scale):
    h = _rmsnorm(x_ref[...], g_ref[...]).astype(BF16)
    cos = cos_ref[...]
    sin_signed = sin_ref[...]
    q_w = n_heads * HEAD_DIM
    kv_w = n_kv * HEAD_DIM

    def partner(t):
        return pltpu.roll(t, HEAD_DIM // 2, axis=1)

    def gained_tables(gain):
        gain_b = _swap_mid_quarters(jnp.broadcast_to(gain, cos.shape))
        return gain_b * cos, partner(gain_b) * sin_signed

    def head_group(out_ref, w_col0, out_col0, tables, out_scale):
        gc, gs = tables
        t = _mm(h, w_ref[:, w_col0:w_col0 + kv_w])
        for hd in range(n_kv):
            th = t[:, hd * HEAD_DIM:(hd + 1) * HEAD_DIM]
            r = lax.rsqrt(jnp.mean(th * th, axis=-1, keepdims=True) + NORM_EPS)
            if out_scale != 1.0:
                r = r * out_scale
            th = (th * gc + partner(th) * gs) * r
            out_ref[:, out_col0 + hd * HEAD_DIM:out_col0 + (hd + 1) * HEAD_DIM] = th.astype(BF16)

    head_group(k_ref, q_w, 0, gained_tables(kn_ref[...]), 1.0)
    q_tables = gained_tables(qn_ref[...])
    for c0 in range(0, q_w, kv_w):
        head_group(q_ref, c0, c0, q_tables, scale)
    v_ref[...] = _mm(h, w_ref[:, q_w + kv_w:]).astype(BF16)


def _qkv(x2, g, w_qkv, q_norm, k_norm, cos, sin_signed, *, bm, seq, casts):
    m, d = x2.shape
    n_kv = N_KV_HEADS
    kv_w = n_kv * HEAD_DIM
    q_w = w_qkv.shape[1] - 2 * kv_w
    n_heads = q_w // HEAD_DIM
    row = lambda i: (i, 0)
    fixed = lambda i: (0, 0)
    pos = lambda i: (i % (seq // bm), 0)
    est = (2 * bm * d * 4 + w_qkv.size * 2 + 4 * bm * HEAD_DIM * 4
           + 2 * bm * (q_w + 2 * kv_w) * 2 + bm * d * 2 + bm * q_w * 4 * 2)
    return _call_with_casts(
        functools.partial(_qkv_kernel, n_heads=n_heads, n_kv=n_kv,
                          scale=HEAD_DIM ** -0.5 * math.log2(math.e)),
        name="attn_qkv", grid=(m // bm,),
        in_specs=[
            pl.BlockSpec((bm, d), row),
            pl.BlockSpec((1, d), fixed),
            pl.BlockSpec(w_qkv.shape, fixed),
            pl.BlockSpec((1, HEAD_DIM), fixed),
            pl.BlockSpec((1, HEAD_DIM), fixed),
            pl.BlockSpec((bm, HEAD_DIM), pos),
            pl.BlockSpec((bm, HEAD_DIM), pos),
        ],
        out_specs=[pl.BlockSpec((bm, q_w), row), pl.BlockSpec((bm, kv_w), row),
                   pl.BlockSpec((bm, kv_w), row)],
        out_shape=[jax.ShapeDtypeStruct((m, q_w), BF16),
                   jax.ShapeDtypeStruct((m, kv_w), BF16),
                   jax.ShapeDtypeStruct((m, kv_w), BF16)],
        args=(x2, g, w_qkv, q_norm, k_norm, cos, sin_signed), est=est, casts=casts)


def _attn_kernel(q_ref, k_ref, v_ref, o_ref, *, n_rep, rows):
    qb = q_ref.shape[0]
    for r in range(n_rep):
        cols = slice(r * HEAD_DIM, (r + 1) * HEAD_DIM)
        for r0 in range(0, qb, rows):
            rs = slice(r0, r0 + rows)
            s = lax.dot_general(q_ref[rs, cols], k_ref[...], (((1,), (1,)), ((), ())),
                                preferred_element_type=F32)
            p = jnp.exp2(s - jnp.max(s, axis=-1, keepdims=True))
            l = jnp.sum(p, axis=-1, keepdims=True)
            o_ref[rs, cols] = (_mm(p.astype(BF16), v_ref[...]) / l).astype(BF16)


def _attention(q, k, v, *, batch, seq, qb, rows, casts):
    m, q_w = q.shape
    n_kv = k.shape[1] // HEAD_DIM
    n_rep = q_w // HEAD_DIM // n_kv
    nb = seq // qb
    n_chains = n_rep * (qb // rows)
    est = (2 * 2 * qb * n_rep * HEAD_DIM * 2 + 2 * 2 * seq * HEAD_DIM * 2
           + max(4, n_chains // 2) * rows * seq * (4 + 2))
    return _call_with_casts(
        functools.partial(_attn_kernel, n_rep=n_rep, rows=rows),
        name="attn_core", grid=(batch, n_kv, nb),
        in_specs=[
            pl.BlockSpec((qb, n_rep * HEAD_DIM), lambda b, g, i: (b * nb + i, g)),
            pl.BlockSpec((seq, HEAD_DIM), lambda b, g, i: (b, g)),
            pl.BlockSpec((seq, HEAD_DIM), lambda b, g, i: (b, g)),
        ],
        out_specs=[pl.BlockSpec((qb, n_rep * HEAD_DIM), lambda b, g, i: (b * nb + i, g))],
        out_shape=[jax.ShapeDtypeStruct((m, q_w), BF16)],
        args=(q, k, v), est=est, casts=casts)


def _proj_kernel(a_ref, w_ref, x_ref, o_ref):
    o_ref[...] = x_ref[...] + _mm(a_ref[...], w_ref[...])


def _proj_residual(a, w, x2, *, bm, casts):
    m, kdim = a.shape
    d = w.shape[1]
    row = lambda i: (i, 0)
    est = 2 * bm * kdim * 2 + w.size * 2 + 2 * 2 * bm * d * 4 + bm * d * 4
    return _call_with_casts(
        _proj_kernel, name="attn_out", grid=(m // bm,),
        in_specs=[pl.BlockSpec((bm, kdim), row), pl.BlockSpec(w.shape, lambda i: (0, 0)),
                  pl.BlockSpec((bm, d), row)],
        out_specs=[pl.BlockSpec((bm, d), row)],
        out_shape=[jax.ShapeDtypeStruct((m, d), F32)],
        args=(a, w, x2), est=est, casts=casts)


def _rope_tables(seq):
    axis_dim = HEAD_DIM // 2
    t = np.arange(seq)
    inv_freq = ROPE_THETA ** (-np.arange(0, axis_dim, 2, dtype=np.float64) / axis_dim)
    ang_r = (t // GRID_W)[:, None] * inv_freq[None, :]
    ang_c = (t % GRID_W)[:, None] * inv_freq[None, :]
    cr, sr, cc, sc = np.cos(ang_r), np.sin(ang_r), np.cos(ang_c), np.sin(ang_c)
    assert cr.shape == (seq, QUARTER)
    return (jnp.asarray(np.concatenate([cr, cc, cr, cc], axis=-1), dtype=F32),
            jnp.asarray(np.concatenate([-sr, -sc, sr, sc], axis=-1), dtype=F32))


def kernel(x, gm_w_in, gm_ln_g, gm_ln_b, gm_w_s, gm_b_s, gm_w_out, attn_w_qkv, attn_q_norm, attn_k_norm, attn_w_o, ffn_w1, ffn_w2, norm_mix, norm_ffn, norm_final):
    batch, seq, d = x.shape
    assert norm_mix.shape[0] == 2, "two layers: gMLP mixer then attention mixer"
    m = batch * seq
    width = gm_w_out.shape[1]
    x2 = x.reshape(m, d)
    cos, sin_signed = _rope_tables(seq)
    bm = 512
    ffn_bm = 1024
    fc = 1024

    (u, v), (w_out, w1_0, w2_0) = _gmlp_in(
        x2, norm_mix[0].reshape(1, d), gm_w_in[0].astype(BF16),
        gm_ln_g[0].reshape(1, width), gm_ln_b[0].reshape(1, width), bm=bm,
        casts=(_Cast(gm_w_out, 0), _Cast(ffn_w1, 0), _Cast(ffn_w2, 0)))
    b_full = jnp.repeat(gm_b_s[0].T, width // gm_b_s.shape[1], axis=1)
    (x2,), (w_qkv,) = _gmlp_out(
        u, v, gm_w_s[0].astype(BF16), b_full, w_out, x2, bm=bm,
        casts=(_Cast(attn_w_qkv, 0, functools.partial(
            _cast_qkv_weights, qk_width=attn_w_qkv.shape[2] - N_KV_HEADS * HEAD_DIM)),))
    x2 = _ffn(x2, norm_ffn[0].reshape(1, d), w1_0, w2_0, norm_final.reshape(1, d),
              bm=ffn_bm, fc=fc, final_norm=False)

    (q, k, v), (w_o, w1_1, w2_1) = _qkv(
        x2, norm_mix[1].reshape(1, d), w_qkv, attn_q_norm[0].reshape(1, HEAD_DIM),
        attn_k_norm[0].reshape(1, HEAD_DIM), cos, sin_signed, bm=bm, seq=seq,
        casts=(_Cast(attn_w_o, 0), _Cast(ffn_w1, 1), _Cast(ffn_w2, 1)))
    (o,), _ = _attention(q, k, v, batch=batch, seq=seq, qb=2048, rows=512, casts=())
    (x2,), _ = _proj_residual(o, w_o, x2, bm=bm, casts=())
    x2 = _ffn(x2, norm_ffn[1].reshape(1, d), w1_1, w2_1, norm_final.reshape(1, d),
              bm=ffn_bm, fc=fc, final_norm=True)
    return x2.reshape(batch, seq, d)
```

```python
import functools
import math
from typing import Callable, NamedTuple, Optional

import jax
import jax.numpy as jnp
import numpy as np
from jax import lax
from jax.experimental import pallas as pl
from jax.experimental.pallas import tpu as pltpu

NORM_EPS = 1e-6
CHUNK = 128
HEAD_DIM = 128
N_KV_HEADS = 4
GRID_W = 64
ROPE_THETA = 10000.0

V7X_VMEM_BYTES = 64 * 1024 * 1024
VMEM_RESERVE_BYTES = 6 * 1024 * 1024
VMEM_TEMP_BYTES = 8 * 1024 * 1024
LANES = 128

BF16 = jnp.bfloat16
F32 = jnp.float32


def _vmem_limit(estimate_bytes):
    return int(min(V7X_VMEM_BYTES - VMEM_RESERVE_BYTES, estimate_bytes + VMEM_TEMP_BYTES))


def _rmsnorm(xf, g):
    ms = jnp.mean(xf * xf, axis=-1, keepdims=True)
    return xf * lax.rsqrt(ms + NORM_EPS) * g


def _gelu(z):
    return 0.5 * z * (1.0 + lax.erf(z * math.sqrt(0.5)))


def _mm(a, b):
    return jnp.dot(a, b, preferred_element_type=F32)


class _Cast(NamedTuple):
    w: jax.Array
    layer: int
    transform: Optional[Callable] = None


def _call_with_casts(body, *, name, grid, in_specs, out_specs, out_shape, args,
                     est, casts=(), scratch_shapes=()):
    n_in, n_out, n_cast = len(in_specs), len(out_specs), len(casts)
    n_steps = math.prod(grid)

    def step(*idx):
        lin = idx[0]
        for i, n in zip(idx[1:], grid[1:]):
            lin = lin * n + i
        return lin

    cast_layers = [c.layer for c in casts]
    cast_shapes = [c.w.shape[1:] for c in casts]
    cast_fns = [c.transform for c in casts]
    casts = [c.w.reshape(-1, c.w.shape[2]) for c in casts]

    def kernel(*refs):
        ins, refs = refs[:n_in], refs[n_in:]
        cast_in, refs = refs[:n_cast], refs[n_cast:]
        outs, refs = refs[:n_out], refs[n_out:]
        cast_out, scratch = refs[:n_cast], refs[n_cast:]
        body(*ins, *outs, *scratch)
        for src, dst, fn in zip(cast_in, cast_out, cast_fns):
            if fn is None:
                dst[...] = src[...].astype(BF16)
            else:
                fn(src, dst)

    cast_in_specs = [
        pl.BlockSpec((r // n_steps, c),
                     functools.partial(lambda *idx, layer: (layer * n_steps + step(*idx), 0),
                                       layer=layer))
        for (r, c), layer in zip(cast_shapes, cast_layers)]
    cast_out_specs = [pl.BlockSpec((r // n_steps, c), lambda *idx: (step(*idx), 0))
                      for r, c in cast_shapes]
    est += sum(2 * (r // n_steps) * c * (4 + 2) for r, c in cast_shapes)
    outs = pl.pallas_call(
        kernel,
        grid=grid,
        in_specs=list(in_specs) + cast_in_specs,
        out_specs=list(out_specs) + cast_out_specs,
        out_shape=list(out_shape) + [jax.ShapeDtypeStruct(s, BF16) for s in cast_shapes],
        scratch_shapes=list(scratch_shapes),
        compiler_params=pltpu.CompilerParams(
            dimension_semantics=("parallel",) * len(grid), vmem_limit_bytes=_vmem_limit(est)),
        name=name,
    )(*args, *casts)
    return outs[:n_out], outs[n_out:]


def _gmlp_in_kernel(x_ref, g_ref, w_ref, lg_ref, lb_ref, u_ref, v_ref, *, width):
    h = _rmsnorm(x_ref[...], g_ref[...]).astype(BF16)
    v = _gelu(_mm(h, w_ref[:, width:]))
    mu = jnp.mean(v, axis=-1, keepdims=True)
    vc = v - mu
    var = jnp.mean(vc * vc, axis=-1, keepdims=True)
    vn = vc * lax.rsqrt(var + NORM_EPS) * lg_ref[...] + lb_ref[...]
    v_ref[...] = vn.astype(BF16)
    u_ref[...] = _gelu(_mm(h, w_ref[:, :width])).astype(BF16)


def _gmlp_in(x2, g, w_in, ln_g, ln_b, *, bm, casts):
    m, d = x2.shape
    width = w_in.shape[1] // 2
    row = lambda i: (i, 0)
    fixed = lambda i: (0, 0)
    est = (2 * bm * d * 4 + w_in.size * 2 + 2 * 2 * bm * width * 2
           + bm * d * 2 + 3 * bm * width * 4)
    return _call_with_casts(
        functools.partial(_gmlp_in_kernel, width=width),
        name="gmlp_in", grid=(m // bm,),
        in_specs=[
            pl.BlockSpec((bm, d), row),
            pl.BlockSpec((1, d), fixed),
            pl.BlockSpec(w_in.shape, fixed),
            pl.BlockSpec((1, width), fixed),
            pl.BlockSpec((1, width), fixed),
        ],
        out_specs=[pl.BlockSpec((bm, width), row), pl.BlockSpec((bm, width), row)],
        out_shape=[jax.ShapeDtypeStruct((m, width), BF16)] * 2,
        args=(x2, g, w_in, ln_g, ln_b), est=est, casts=casts)


def _gmlp_out_kernel(u_ref, v_ref, ws_ref, bs_ref, wout_ref, x_ref, o_ref, y_ref, *, groups):
    bm = u_ref.shape[0]
    n_chunks = bm // CHUNK
    for g in range(groups):
        cols = slice(g * LANES, (g + 1) * LANES)
        rhs = jnp.concatenate(
            [v_ref[c * CHUNK:(c + 1) * CHUNK, cols] for c in range(n_chunks)], axis=1)
        sg = _mm(ws_ref[g], rhs)
        bias = bs_ref[:, cols]
        for c in range(n_chunks):
            rows = slice(c * CHUNK, (c + 1) * CHUNK)
            s = sg[:, c * LANES:(c + 1) * LANES] + bias
            y_ref[rows, cols] = (u_ref[rows, cols].astype(F32) * s).astype(BF16)
    o_ref[...] = x_ref[...] + _mm(y_ref[...], wout_ref[...])


def _gmlp_out(u, v, w_s, b_full, w_out, x2, *, bm, casts):
    m, width = u.shape
    d = w_out.shape[1]
    groups = w_s.shape[0]
    row = lambda i: (i, 0)
    est = (2 * 2 * bm * width * 2 + w_s.size * 2 + b_full.size * 4 + w_out.size * 2
           + 2 * 2 * bm * d * 4 + bm * width * 2 + bm * d * 4)
    return _call_with_casts(
        functools.partial(_gmlp_out_kernel, groups=groups),
        name="gmlp_out", grid=(m // bm,),
        in_specs=[
            pl.BlockSpec((bm, width), row),
            pl.BlockSpec((bm, width), row),
            pl.BlockSpec(w_s.shape, lambda i: (0, 0, 0)),
            pl.BlockSpec(b_full.shape, lambda i: (0, 0)),
            pl.BlockSpec(w_out.shape, lambda i: (0, 0)),
            pl.BlockSpec((bm, d), row),
        ],
        out_specs=[pl.BlockSpec((bm, d), row)],
        out_shape=[jax.ShapeDtypeStruct((m, d), F32)],
        scratch_shapes=[pltpu.VMEM((bm, width), BF16)],
        args=(u, v, w_s, b_full, w_out, x2), est=est, casts=casts)


def _ffn_kernel(x_ref, g_ref, w1_hbm, w2_hbm, gf_ref, o_ref, h_ref, w1_buf, w2_buf, sem,
                *, n_chunks, fc, final_norm):
    i = pl.program_id(0)
    last_tile = pl.num_programs(0) - 1

    def w1_copy(chunk):
        c, slot = chunk % n_chunks, chunk % 2
        return pltpu.make_async_copy(w1_hbm.at[:, pl.ds(c * fc, fc)], w1_buf.at[slot],
                                     sem.at[0, slot])

    def w2_copy(chunk):
        c, slot = chunk % n_chunks, chunk % 2
        return pltpu.make_async_copy(w2_hbm.at[pl.ds(c * fc, fc), :], w2_buf.at[slot],
                                     sem.at[1, slot])

    def start(copy, chunk):
        if chunk < n_chunks:
            copy(chunk).start()
        else:
            @pl.when(i < last_tile)
            def _():
                copy(chunk).start()

    @pl.when(i == 0)
    def _():
        w1_copy(0).start()
        w1_copy(1).start()
        w2_copy(0).start()

    xf = x_ref[...]
    h_ref[...] = _rmsnorm(xf, g_ref[...]).astype(BF16)
    w1_copy(0).wait()
    for j in range(n_chunks):
        a = jnp.maximum(_mm(h_ref[...], w1_buf[j % 2]), 0.0)
        a = (a * a).astype(BF16)
        if j + 1 < n_chunks:
            w1_copy(j + 1).wait()
        w2_copy(j).wait()
        start(w1_copy, j + 2)
        start(w2_copy, j + 1)
        acc = xf if j == 0 else o_ref[...]
        o_ref[...] = acc + _mm(a, w2_buf[j % 2])

    if final_norm:
        o_ref[...] = _rmsnorm(o_ref[...], gf_ref[...])


def _ffn(x2, g, w1, w2, g_final, *, bm, fc, final_norm):
    m, d = x2.shape
    d_ff = w1.shape[1]
    n_chunks = d_ff // fc
    assert n_chunks % 2 == 0 and n_chunks >= 2, "slot of a chunk must be the same in every tile"
    est = (2 * bm * d * 4 + 2 * d * fc * 2 + 2 * fc * d * 2 + 2 * bm * d * 4
           + bm * d * 2 + bm * fc * 4 + bm * fc * 2)
    return pl.pallas_call(
        functools.partial(_ffn_kernel, n_chunks=n_chunks, fc=fc, final_norm=final_norm),
        grid=(m // bm,),
        in_specs=[
            pl.BlockSpec((bm, d), lambda i: (i, 0)),
            pl.BlockSpec((1, d), lambda i: (0, 0)),
            pl.BlockSpec(memory_space=pl.ANY),
            pl.BlockSpec(memory_space=pl.ANY),
            pl.BlockSpec((1, d), lambda i: (0, 0)),
        ],
        out_specs=pl.BlockSpec((bm, d), lambda i: (i, 0)),
        out_shape=jax.ShapeDtypeStruct((m, d), F32),
        scratch_shapes=[pltpu.VMEM((bm, d), BF16),
                        pltpu.VMEM((2, d, fc), BF16),
                        pltpu.VMEM((2, fc, d), BF16),
                        pltpu.SemaphoreType.DMA((2, 2))],
        compiler_params=pltpu.CompilerParams(
            dimension_semantics=("arbitrary",),
            vmem_limit_bytes=_vmem_limit(est)),
        name="ffn_final" if final_norm else "ffn",
    )(x2, g, w1, w2, g_final)


QUARTER = HEAD_DIM // 4


def _swap_mid_quarters(t):
    lane = lax.broadcasted_iota(jnp.int32, t.shape, 1)
    from_right = (lane >= QUARTER) & (lane < 2 * QUARTER)
    from_left = (lane >= 2 * QUARTER) & (lane < 3 * QUARTER)
    return jnp.where(from_right, pltpu.roll(t, HEAD_DIM - QUARTER, axis=1),
                     jnp.where(from_left, pltpu.roll(t, QUARTER, axis=1), t))


def _cast_qkv_weights(src_ref, dst_ref, *, qk_width):
    for c0 in range(0, qk_width, HEAD_DIM):
        cols = slice(c0, c0 + HEAD_DIM)
        dst_ref[:, cols] = _swap_mid_quarters(src_ref[:, cols]).astype(BF16)
    dst_ref[:, qk_width:] = src_ref[:, qk_width:].astype(BF16)


def _qkv_kernel(x_ref, g_ref, w_ref, qn_ref, kn_ref, cos_ref, sin_ref,
                q_ref, k_ref, v_ref, *, n_heads, n_kv, scale):
    h = _rmsnorm(x_ref[...], g_ref[...]).astype(BF16)
    cos = cos_ref[...]
    sin_signed = sin_ref[...]
    q_w = n_heads * HEAD_DIM
    kv_w = n_kv * HEAD_DIM

    def partner(t):
        return pltpu.roll(t, HEAD_DIM // 2, axis=1)

    def gained_tables(gain):
        gain_b = _swap_mid_quarters(jnp.broadcast_to(gain, cos.shape))
        return gain_b * cos, partner(gain_b) * sin_signed

    def head_group(out_ref, w_col0, out_col0, tables, out_scale):
        gc, gs = tables
        t = _mm(h, w_ref[:, w_col0:w_col0 + kv_w])
        for hd in range(n_kv):
            th = t[:, hd * HEAD_DIM:(hd + 1) * HEAD_DIM]
            r = lax.rsqrt(jnp.mean(th * th, axis=-1, keepdims=True) + NORM_EPS)
            if out_scale != 1.0:
                r = r * out_scale
            th = (th * gc + partner(th) * gs) * r
            out_ref[:, out_col0 + hd * HEAD_DIM:out_col0 + (hd + 1) * HEAD_DIM] = th.astype(BF16)

    head_group(k_ref, q_w, 0, gained_tables(kn_ref[...]), 1.0)
    q_tables = gained_tables(qn_ref[...])
    for c0 in range(0, q_w, kv_w):
        head_group(q_ref, c0, c0, q_tables, scale)
    v_ref[...] = _mm(h, w_ref[:, q_w + kv_w:]).astype(BF16)


def _qkv(x2, g, w_qkv, q_norm, k_norm, cos, sin_signed, *, bm, seq, casts):
    m, d = x2.shape
    n_kv = N_KV_HEADS
    kv_w = n_kv * HEAD_DIM
    q_w = w_qkv.shape[1] - 2 * kv_w
    n_heads = q_w // HEAD_DIM
    row = lambda i: (i, 0)
    fixed = lambda i: (0, 0)
    pos = lambda i: (i % (seq // bm), 0)
    est = (2 * bm * d * 4 + w_qkv.size * 2 + 4 * bm * HEAD_DIM * 4
           + 2 * bm * (q_w + 2 * kv_w) * 2 + bm * d * 2 + bm * q_w * 4 * 2)
    return _call_with_casts(
        functools.partial(_qkv_kernel, n_heads=n_heads, n_kv=n_kv,
                          scale=HEAD_DIM ** -0.5 * math.log2(math.e)),
        name="attn_qkv", grid=(m // bm,),
        in_specs=[
            pl.BlockSpec((bm, d), row),
            pl.BlockSpec((1, d), fixed),
            pl.BlockSpec(w_qkv.shape, fixed),
            pl.BlockSpec((1, HEAD_DIM), fixed),
            pl.BlockSpec((1, HEAD_DIM), fixed),
            pl.BlockSpec((bm, HEAD_DIM), pos),
            pl.BlockSpec((bm, HEAD_DIM), pos),
        ],
        out_specs=[pl.BlockSpec((bm, q_w), row), pl.BlockSpec((bm, kv_w), row),
                   pl.BlockSpec((bm, kv_w), row)],
        out_shape=[jax.ShapeDtypeStruct((m, q_w), BF16),
                   jax.ShapeDtypeStruct((m, kv_w), BF16),
                   jax.ShapeDtypeStruct((m, kv_w), BF16)],
        args=(x2, g, w_qkv, q_norm, k_norm, cos, sin_signed), est=est, casts=casts)


def _attn_kernel(q_ref, k_ref, v_ref, o_ref, *, n_rep, rows):
    qb = q_ref.shape[0]
    for r in range(n_rep):
        cols = slice(r * HEAD_DIM, (r + 1) * HEAD_DIM)
        for r0 in range(0, qb, rows):
            rs = slice(r0, r0 + rows)
            s = lax.dot_general(q_ref[rs, cols], k_ref[...], (((1,), (1,)), ((), ())),
                                preferred_element_type=F32)
            p = jnp.exp2(s - jnp.max(s, axis=-1, keepdims=True))
            l = jnp.sum(p, axis=-1, keepdims=True)
            o_ref[rs, cols] = (_mm(p.astype(BF16), v_ref[...]) / l).astype(BF16)


def _attention(q, k, v, *, batch, seq, qb, rows, casts):
    m, q_w = q.shape
    n_kv = k.shape[1] // HEAD_DIM
    n_rep = q_w // HEAD_DIM // n_kv
    nb = seq // qb
    n_chains = n_rep * (qb // rows)
    est = (2 * 2 * qb * n_rep * HEAD_DIM * 2 + 2 * 2 * seq * HEAD_DIM * 2
           + max(4, n_chains // 2) * rows * seq * (4 + 2))
    return _call_with_casts(
        functools.partial(_attn_kernel, n_rep=n_rep, rows=rows),
        name="attn_core", grid=(batch, n_kv, nb),
        in_specs=[
            pl.BlockSpec((qb, n_rep * HEAD_DIM), lambda b, g, i: (b * nb + i, g)),
            pl.BlockSpec((seq, HEAD_DIM), lambda b, g, i: (b, g)),
            pl.BlockSpec((seq, HEAD_DIM), lambda b, g, i: (b, g)),
        ],
        out_specs=[pl.BlockSpec((qb, n_rep * HEAD_DIM), lambda b, g, i: (b * nb + i, g))],
        out_shape=[jax.ShapeDtypeStruct((m, q_w), BF16)],
        args=(q, k, v), est=est, casts=casts)


def _proj_kernel(a_ref, w_ref, x_ref, o_ref):
    o_ref[...] = x_ref[...] + _mm(a_ref[...], w_ref[...])


def _proj_residual(a, w, x2, *, bm, casts):
    m, kdim = a.shape
    d = w.shape[1]
    row = lambda i: (i, 0)
    est = 2 * bm * kdim * 2 + w.size * 2 + 2 * 2 * bm * d * 4 + bm * d * 4
    return _call_with_casts(
        _proj_kernel, name="attn_out", grid=(m // bm,),
        in_specs=[pl.BlockSpec((bm, kdim), row), pl.BlockSpec(w.shape, lambda i: (0, 0)),
                  pl.BlockSpec((bm, d), row)],
        out_specs=[pl.BlockSpec((bm, d), row)],
        out_shape=[jax.ShapeDtypeStruct((m, d), F32)],
        args=(a, w, x2), est=est, casts=casts)


def _rope_tables(seq):
    axis_dim = HEAD_DIM // 2
    t = np.arange(seq)
    inv_freq = ROPE_THETA ** (-np.arange(0, axis_dim, 2, dtype=np.float64) / axis_dim)
    ang_r = (t // GRID_W)[:, None] * inv_freq[None, :]
    ang_c = (t % GRID_W)[:, None] * inv_freq[None, :]
    cr, sr, cc, sc = np.cos(ang_r), np.sin(ang_r), np.cos(ang_c), np.sin(ang_c)
    assert cr.shape == (seq, QUARTER)
    return (jnp.asarray(np.concatenate([cr, cc, cr, cc], axis=-1), dtype=F32),
            jnp.asarray(np.concatenate([-sr, -sc, sr, sc], axis=-1), dtype=F32))


def kernel(x, gm_w_in, gm_ln_g, gm_ln_b, gm_w_s, gm_b_s, gm_w_out, attn_w_qkv, attn_q_norm, attn_k_norm, attn_w_o, ffn_w1, ffn_w2, norm_mix, norm_ffn, norm_final):
    batch, seq, d = x.shape
    assert norm_mix.shape[0] == 2, "two layers: gMLP mixer then attention mixer"
    m = batch * seq
    width = gm_w_out.shape[1]
    x2 = x.reshape(m, d)
    cos, sin_signed = _rope_tables(seq)
    bm = 512
    fc = 2048

    (u, v), (w_out, w1_0, w2_0) = _gmlp_in(
        x2, norm_mix[0].reshape(1, d), gm_w_in[0].astype(BF16),
        gm_ln_g[0].reshape(1, width), gm_ln_b[0].reshape(1, width), bm=bm,
        casts=(_Cast(gm_w_out, 0), _Cast(ffn_w1, 0), _Cast(ffn_w2, 0)))
    b_full = jnp.repeat(gm_b_s[0].T, width // gm_b_s.shape[1], axis=1)
    (x2,), (w_qkv,) = _gmlp_out(
        u, v, gm_w_s[0].astype(BF16), b_full, w_out, x2, bm=bm,
        casts=(_Cast(attn_w_qkv, 0, functools.partial(
            _cast_qkv_weights, qk_width=attn_w_qkv.shape[2] - N_KV_HEADS * HEAD_DIM)),))
    x2 = _ffn(x2, norm_ffn[0].reshape(1, d), w1_0, w2_0, norm_final.reshape(1, d),
              bm=bm, fc=fc, final_norm=False)

    (q, k, v), (w_o, w1_1, w2_1) = _qkv(
        x2, norm_mix[1].reshape(1, d), w_qkv, attn_q_norm[0].reshape(1, HEAD_DIM),
        attn_k_norm[0].reshape(1, HEAD_DIM), cos, sin_signed, bm=bm, seq=seq,
        casts=(_Cast(attn_w_o, 0), _Cast(ffn_w1, 1), _Cast(ffn_w2, 1)))
    (o,), _ = _attention(q, k, v, batch=batch, seq=seq, qb=2048, rows=512, casts=())
    (x2,), _ = _proj_residual(o, w_o, x2, bm=bm, casts=())
    x2 = _ffn(x2, norm_ffn[1].reshape(1, d), w1_1, w2_1, norm_final.reshape(1, d),
              bm=bm, fc=fc, final_norm=True)
    return x2.reshape(batch, seq, d)
```

```python
import functools
import math
from typing import Callable, NamedTuple, Optional

import jax
import jax.numpy as jnp
import numpy as np
from jax import lax
from jax.experimental import pallas as pl
from jax.experimental.pallas import tpu as pltpu

NORM_EPS = 1e-6
CHUNK = 128
HEAD_DIM = 128
N_KV_HEADS = 4
GRID_W = 64
ROPE_THETA = 10000.0

V7X_VMEM_BYTES = 64 * 1024 * 1024
VMEM_RESERVE_BYTES = 6 * 1024 * 1024
VMEM_TEMP_BYTES = 8 * 1024 * 1024
LANES = 128

BF16 = jnp.bfloat16
F32 = jnp.float32


def _vmem_limit(estimate_bytes):
    return int(min(V7X_VMEM_BYTES - VMEM_RESERVE_BYTES, estimate_bytes + VMEM_TEMP_BYTES))


def _rmsnorm(xf, g):
    ms = jnp.mean(xf * xf, axis=-1, keepdims=True)
    return xf * lax.rsqrt(ms + NORM_EPS) * g


def _gelu(z):
    return 0.5 * z * (1.0 + lax.erf(z * math.sqrt(0.5)))


def _mm(a, b):
    return jnp.dot(a, b, preferred_element_type=F32)


class _Cast(NamedTuple):
    w: jax.Array
    layer: int
    transform: Optional[Callable] = None


def _call_with_casts(body, *, name, grid, in_specs, out_specs, out_shape, args,
                     est, casts=(), scratch_shapes=()):
    n_in, n_out, n_cast = len(in_specs), len(out_specs), len(casts)
    n_steps = math.prod(grid)

    def step(*idx):
        lin = idx[0]
        for i, n in zip(idx[1:], grid[1:]):
            lin = lin * n + i
        return lin

    cast_layers = [c.layer for c in casts]
    cast_shapes = [c.w.shape[1:] for c in casts]
    cast_fns = [c.transform for c in casts]
    casts = [c.w.reshape(-1, c.w.shape[2]) for c in casts]

    def kernel(*refs):
        ins, refs = refs[:n_in], refs[n_in:]
        cast_in, refs = refs[:n_cast], refs[n_cast:]
        outs, refs = refs[:n_out], refs[n_out:]
        cast_out, scratch = refs[:n_cast], refs[n_cast:]
        body(*ins, *outs, *scratch)
        for src, dst, fn in zip(cast_in, cast_out, cast_fns):
            if fn is None:
                dst[...] = src[...].astype(BF16)
            else:
                fn(src, dst)

    cast_in_specs = [
        pl.BlockSpec((r // n_steps, c),
                     functools.partial(lambda *idx, layer: (layer * n_steps + step(*idx), 0),
                                       layer=layer))
        for (r, c), layer in zip(cast_shapes, cast_layers)]
    cast_out_specs = [pl.BlockSpec((r // n_steps, c), lambda *idx: (step(*idx), 0))
                      for r, c in cast_shapes]
    est += sum(2 * (r // n_steps) * c * (4 + 2) for r, c in cast_shapes)
    outs = pl.pallas_call(
        kernel,
        grid=grid,
        in_specs=list(in_specs) + cast_in_specs,
        out_specs=list(out_specs) + cast_out_specs,
        out_shape=list(out_shape) + [jax.ShapeDtypeStruct(s, BF16) for s in cast_shapes],
        scratch_shapes=list(scratch_shapes),
        compiler_params=pltpu.CompilerParams(
            dimension_semantics=("parallel",) * len(grid), vmem_limit_bytes=_vmem_limit(est)),
        name=name,
    )(*args, *casts)
    return outs[:n_out], outs[n_out:]


def _gmlp_in_kernel(x_ref, g_ref, w_ref, lg_ref, lb_ref, u_ref, v_ref, *, width):
    h = _rmsnorm(x_ref[...], g_ref[...]).astype(BF16)
    v = _gelu(_mm(h, w_ref[:, width:]))
    mu = jnp.mean(v, axis=-1, keepdims=True)
    vc = v - mu
    var = jnp.mean(vc * vc, axis=-1, keepdims=True)
    vn = vc * lax.rsqrt(var + NORM_EPS) * lg_ref[...] + lb_ref[...]
    v_ref[...] = vn.astype(BF16)
    u_ref[...] = _gelu(_mm(h, w_ref[:, :width])).astype(BF16)


def _gmlp_in(x2, g, w_in, ln_g, ln_b, *, bm, casts):
    m, d = x2.shape
    width = w_in.shape[1] // 2
    row = lambda i: (i, 0)
    fixed = lambda i: (0, 0)
    est = (2 * bm * d * 4 + w_in.size * 2 + 2 * 2 * bm * width * 2
           + bm * d * 2 + 3 * bm * width * 4)
    return _call_with_casts(
        functools.partial(_gmlp_in_kernel, width=width),
        name="gmlp_in", grid=(m // bm,),
        in_specs=[
            pl.BlockSpec((bm, d), row),
            pl.BlockSpec((1, d), fixed),
            pl.BlockSpec(w_in.shape, fixed),
            pl.BlockSpec((1, width), fixed),
            pl.BlockSpec((1, width), fixed),
        ],
        out_specs=[pl.BlockSpec((bm, width), row), pl.BlockSpec((bm, width), row)],
        out_shape=[jax.ShapeDtypeStruct((m, width), BF16)] * 2,
        args=(x2, g, w_in, ln_g, ln_b), est=est, casts=casts)


def _gmlp_out_kernel(u_ref, v_ref, ws_ref, bs_ref, wout_ref, x_ref, o_ref, y_ref, *, groups):
    bm = u_ref.shape[0]
    n_chunks = bm // CHUNK
    for g in range(groups):
        cols = slice(g * LANES, (g + 1) * LANES)
        rhs = jnp.concatenate(
            [v_ref[c * CHUNK:(c + 1) * CHUNK, cols] for c in range(n_chunks)], axis=1)
        sg = _mm(ws_ref[g], rhs)
        bias = bs_ref[:, cols]
        for c in range(n_chunks):
            rows = slice(c * CHUNK, (c + 1) * CHUNK)
            s = sg[:, c * LANES:(c + 1) * LANES] + bias
            y_ref[rows, cols] = (u_ref[rows, cols].astype(F32) * s).astype(BF16)
    o_ref[...] = x_ref[...] + _mm(y_ref[...], wout_ref[...])


def _gmlp_out(u, v, w_s, b_full, w_out, x2, *, bm, casts):
    m, width = u.shape
    d = w_out.shape[1]
    groups = w_s.shape[0]
    row = lambda i: (i, 0)
    est = (2 * 2 * bm * width * 2 + w_s.size * 2 + b_full.size * 4 + w_out.size * 2
           + 2 * 2 * bm * d * 4 + bm * width * 2 + bm * d * 4)
    return _call_with_casts(
        functools.partial(_gmlp_out_kernel, groups=groups),
        name="gmlp_out", grid=(m // bm,),
        in_specs=[
            pl.BlockSpec((bm, width), row),
            pl.BlockSpec((bm, width), row),
            pl.BlockSpec(w_s.shape, lambda i: (0, 0, 0)),
            pl.BlockSpec(b_full.shape, lambda i: (0, 0)),
            pl.BlockSpec(w_out.shape, lambda i: (0, 0)),
            pl.BlockSpec((bm, d), row),
        ],
        out_specs=[pl.BlockSpec((bm, d), row)],
        out_shape=[jax.ShapeDtypeStruct((m, d), F32)],
        scratch_shapes=[pltpu.VMEM((bm, width), BF16)],
        args=(u, v, w_s, b_full, w_out, x2), est=est, casts=casts)


def _ffn_kernel(x_ref, g_ref, w1_ref, w2_ref, gf_ref, o_ref, h_ref, *, final_norm):
    j = pl.program_id(1)

    def chunk(h, acc):
        a = jnp.maximum(_mm(h, w1_ref[...]), 0.0)
        o_ref[...] = acc + _mm((a * a).astype(BF16), w2_ref[...])

    @pl.when(j == 0)
    def _():
        xf = x_ref[...]
        h_ref[...] = _rmsnorm(xf, g_ref[...]).astype(BF16)
        chunk(h_ref[...], xf)

    @pl.when(j > 0)
    def _():
        chunk(h_ref[...], o_ref[...])

    if final_norm:
        @pl.when(j == pl.num_programs(1) - 1)
        def _():
            o_ref[...] = _rmsnorm(o_ref[...], gf_ref[...])


def _ffn(x2, g, w1, w2, g_final, *, bm, fc, final_norm):
    m, d = x2.shape
    d_ff = w1.shape[1]
    est = (2 * bm * d * 4 + 2 * d * fc * 2 + 2 * fc * d * 2 + 2 * bm * d * 4
           + bm * d * 2 + bm * fc * 4 + bm * fc * 2 + bm * d * 4)
    return pl.pallas_call(
        functools.partial(_ffn_kernel, final_norm=final_norm),
        grid=(m // bm, d_ff // fc),
        in_specs=[
            pl.BlockSpec((bm, d), lambda i, j: (i, 0)),
            pl.BlockSpec((1, d), lambda i, j: (0, 0)),
            pl.BlockSpec((d, fc), lambda i, j: (0, j)),
            pl.BlockSpec((fc, d), lambda i, j: (j, 0)),
            pl.BlockSpec((1, d), lambda i, j: (0, 0)),
        ],
        out_specs=pl.BlockSpec((bm, d), lambda i, j: (i, 0)),
        out_shape=jax.ShapeDtypeStruct((m, d), F32),
        scratch_shapes=[pltpu.VMEM((bm, d), BF16)],
        compiler_params=pltpu.CompilerParams(
            dimension_semantics=("parallel", "arbitrary"),
            vmem_limit_bytes=_vmem_limit(est)),
        name="ffn_final" if final_norm else "ffn",
    )(x2, g, w1, w2, g_final)


QUARTER = HEAD_DIM // 4


def _swap_mid_quarters(t):
    lane = lax.broadcasted_iota(jnp.int32, t.shape, 1)
    from_right = (lane >= QUARTER) & (lane < 2 * QUARTER)
    from_left = (lane >= 2 * QUARTER) & (lane < 3 * QUARTER)
    return jnp.where(from_right, pltpu.roll(t, HEAD_DIM - QUARTER, axis=1),
                     jnp.where(from_left, pltpu.roll(t, QUARTER, axis=1), t))


def _cast_qkv_weights(src_ref, dst_ref, *, qk_width):
    for c0 in range(0, qk_width, HEAD_DIM):
        cols = slice(c0, c0 + HEAD_DIM)
        dst_ref[:, cols] = _swap_mid_quarters(src_ref[:, cols]).astype(BF16)
    dst_ref[:, qk_width:] = src_ref[:, qk_width:].astype(BF16)


def _qkv_kernel(x_ref, g_ref, w_ref, qn_ref, kn_ref, cos_ref, sin_ref,
                q_ref, k_ref, v_ref, *, n_heads, n_kv, scale):
    h = _rmsnorm(x_ref[...], g_ref[...]).astype(BF16)
    cos = cos_ref[...]
    sin_signed = sin_ref[...]
    q_w = n_heads * HEAD_DIM
    kv_w = n_kv * HEAD_DIM

    def partner(t):
        return pltpu.roll(t, HEAD_DIM // 2, axis=1)

    def gained_tables(gain):
        gain_b = _swap_mid_quarters(jnp.broadcast_to(gain, cos.shape))
        return gain_b * cos, partner(gain_b) * sin_signed

    def head_group(out_ref, w_col0, out_col0, tables, out_scale):
        gc, gs = tables
        t = _mm(h, w_ref[:, w_col0:w_col0 + kv_w])
        for hd in range(n_kv):
            th = t[:, hd * HEAD_DIM:(hd + 1) * HEAD_DIM]
            r = lax.rsqrt(jnp.mean(th * th, axis=-1, keepdims=True) + NORM_EPS)
            if out_scale != 1.0:
                r = r * out_scale
            th = (th * gc + partner(th) * gs) * r
            out_ref[:, out_col0 + hd * HEAD_DIM:out_col0 + (hd + 1) * HEAD_DIM] = th.astype(BF16)

    head_group(k_ref, q_w, 0, gained_tables(kn_ref[...]), 1.0)
    q_tables = gained_tables(qn_ref[...])
    for c0 in range(0, q_w, kv_w):
        head_group(q_ref, c0, c0, q_tables, scale)
    v_ref[...] = _mm(h, w_ref[:, q_w + kv_w:]).astype(BF16)


def _qkv(x2, g, w_qkv, q_norm, k_norm, cos, sin_signed, *, bm, seq, casts):
    m, d = x2.shape
    n_kv = N_KV_HEADS
    kv_w = n_kv * HEAD_DIM
    q_w = w_qkv.shape[1] - 2 * kv_w
    n_heads = q_w // HEAD_DIM
    row = lambda i: (i, 0)
    fixed = lambda i: (0, 0)
    pos = lambda i: (i % (seq // bm), 0)
    est = (2 * bm * d * 4 + w_qkv.size * 2 + 4 * bm * HEAD_DIM * 4
           + 2 * bm * (q_w + 2 * kv_w) * 2 + bm * d * 2 + bm * q_w * 4 * 2)
    return _call_with_casts(
        functools.partial(_qkv_kernel, n_heads=n_heads, n_kv=n_kv,
                          scale=HEAD_DIM ** -0.5 * math.log2(math.e)),
        name="attn_qkv", grid=(m // bm,),
        in_specs=[
            pl.BlockSpec((bm, d), row),
            pl.BlockSpec((1, d), fixed),
            pl.BlockSpec(w_qkv.shape, fixed),
            pl.BlockSpec((1, HEAD_DIM), fixed),
            pl.BlockSpec((1, HEAD_DIM), fixed),
            pl.BlockSpec((bm, HEAD_DIM), pos),
            pl.BlockSpec((bm, HEAD_DIM), pos),
        ],
        out_specs=[pl.BlockSpec((bm, q_w), row), pl.BlockSpec((bm, kv_w), row),
                   pl.BlockSpec((bm, kv_w), row)],
        out_shape=[jax.ShapeDtypeStruct((m, q_w), BF16),
                   jax.ShapeDtypeStruct((m, kv_w), BF16),
                   jax.ShapeDtypeStruct((m, kv_w), BF16)],
        args=(x2, g, w_qkv, q_norm, k_norm, cos, sin_signed), est=est, casts=casts)


def _attn_kernel(q_ref, k_ref, v_ref, o_ref, *, n_rep, rows):
    qb = q_ref.shape[0]
    v = v_ref[...]
    v_ones = jnp.concatenate([v, jnp.ones_like(v)], axis=1)
    for r in range(n_rep):
        cols = slice(r * HEAD_DIM, (r + 1) * HEAD_DIM)
        for r0 in range(0, qb, rows):
            rs = slice(r0, r0 + rows)
            s = lax.dot_general(q_ref[rs, cols], k_ref[...], (((1,), (1,)), ((), ())),
                                preferred_element_type=F32)
            p = jnp.exp2(s - jnp.max(s, axis=-1, keepdims=True))
            pv = _mm(p.astype(BF16), v_ones)
            o_ref[rs, cols] = (pv[:, :HEAD_DIM] / pv[:, HEAD_DIM:]).astype(BF16)


def _attention(q, k, v, *, batch, seq, qb, rows, casts):
    m, q_w = q.shape
    n_kv = k.shape[1] // HEAD_DIM
    n_rep = q_w // HEAD_DIM // n_kv
    nb = seq // qb
    est = (2 * 2 * qb * n_rep * HEAD_DIM * 2 + 2 * 2 * seq * HEAD_DIM * 2
           + seq * 2 * HEAD_DIM * 2 + 4 * rows * seq * (4 + 2))
    return _call_with_casts(
        functools.partial(_attn_kernel, n_rep=n_rep, rows=rows),
        name="attn_core", grid=(batch, n_kv, nb),
        in_specs=[
            pl.BlockSpec((qb, n_rep * HEAD_DIM), lambda b, g, i: (b * nb + i, g)),
            pl.BlockSpec((seq, HEAD_DIM), lambda b, g, i: (b, g)),
            pl.BlockSpec((seq, HEAD_DIM), lambda b, g, i: (b, g)),
        ],
        out_specs=[pl.BlockSpec((qb, n_rep * HEAD_DIM), lambda b, g, i: (b * nb + i, g))],
        out_shape=[jax.ShapeDtypeStruct((m, q_w), BF16)],
        args=(q, k, v), est=est, casts=casts)


def _proj_kernel(a_ref, w_ref, x_ref, o_ref):
    o_ref[...] = x_ref[...] + _mm(a_ref[...], w_ref[...])


def _proj_residual(a, w, x2, *, bm, casts):
    m, kdim = a.shape
    d = w.shape[1]
    row = lambda i: (i, 0)
    est = 2 * bm * kdim * 2 + w.size * 2 + 2 * 2 * bm * d * 4 + bm * d * 4
    return _call_with_casts(
        _proj_kernel, name="attn_out", grid=(m // bm,),
        in_specs=[pl.BlockSpec((bm, kdim), row), pl.BlockSpec(w.shape, lambda i: (0, 0)),
                  pl.BlockSpec((bm, d), row)],
        out_specs=[pl.BlockSpec((bm, d), row)],
        out_shape=[jax.ShapeDtypeStruct((m, d), F32)],
        args=(a, w, x2), est=est, casts=casts)


def _rope_tables(seq):
    axis_dim = HEAD_DIM // 2
    t = np.arange(seq)
    inv_freq = ROPE_THETA ** (-np.arange(0, axis_dim, 2, dtype=np.float64) / axis_dim)
    ang_r = (t // GRID_W)[:, None] * inv_freq[None, :]
    ang_c = (t % GRID_W)[:, None] * inv_freq[None, :]
    cr, sr, cc, sc = np.cos(ang_r), np.sin(ang_r), np.cos(ang_c), np.sin(ang_c)
    assert cr.shape == (seq, QUARTER)
    return (jnp.asarray(np.concatenate([cr, cc, cr, cc], axis=-1), dtype=F32),
            jnp.asarray(np.concatenate([-sr, -sc, sr, sc], axis=-1), dtype=F32))


def kernel(x, gm_w_in, gm_ln_g, gm_ln_b, gm_w_s, gm_b_s, gm_w_out, attn_w_qkv, attn_q_norm, attn_k_norm, attn_w_o, ffn_w1, ffn_w2, norm_mix, norm_ffn, norm_final):
    batch, seq, d = x.shape
    assert norm_mix.shape[0] == 2, "two layers: gMLP mixer then attention mixer"
    m = batch * seq
    width = gm_w_out.shape[1]
    x2 = x.reshape(m, d)
    cos, sin_signed = _rope_tables(seq)
    bm = 512
    fc = 2048

    (u, v), (w_out, w1_0, w2_0) = _gmlp_in(
        x2, norm_mix[0].reshape(1, d), gm_w_in[0].astype(BF16),
        gm_ln_g[0].reshape(1, width), gm_ln_b[0].reshape(1, width), bm=bm,
        casts=(_Cast(gm_w_out, 0), _Cast(ffn_w1, 0), _Cast(ffn_w2, 0)))
    b_full = jnp.repeat(gm_b_s[0].T, width // gm_b_s.shape[1], axis=1)
    (x2,), (w_qkv,) = _gmlp_out(
        u, v, gm_w_s[0].astype(BF16), b_full, w_out, x2, bm=bm,
        casts=(_Cast(attn_w_qkv, 0, functools.partial(
            _cast_qkv_weights, qk_width=attn_w_qkv.shape[2] - N_KV_HEADS * HEAD_DIM)),))
    x2 = _ffn(x2, norm_ffn[0].reshape(1, d), w1_0, w2_0, norm_final.reshape(1, d),
              bm=bm, fc=fc, final_norm=False)

    (q, k, v), (w_o, w1_1, w2_1) = _qkv(
        x2, norm_mix[1].reshape(1, d), w_qkv, attn_q_norm[0].reshape(1, HEAD_DIM),
        attn_k_norm[0].reshape(1, HEAD_DIM), cos, sin_signed, bm=bm, seq=seq,
        casts=(_Cast(attn_w_o, 0), _Cast(ffn_w1, 1), _Cast(ffn_w2, 1)))
    (o,), _ = _attention(q, k, v, batch=batch, seq=seq, qb=2048, rows=128, casts=())
    (x2,), _ = _proj_residual(o, w_o, x2, bm=bm, casts=())
    x2 = _ffn(x2, norm_ffn[1].reshape(1, d), w1_1, w2_1, norm_final.reshape(1, d),
              bm=bm, fc=fc, final_norm=True)
    return x2.reshape(batch, seq, d)
```

```python
import functools
import math
from typing import Callable, NamedTuple, Optional

import jax
import jax.numpy as jnp
import numpy as np
from jax import lax
from jax.experimental import pallas as pl
from jax.experimental.pallas import tpu as pltpu

NORM_EPS = 1e-6
CHUNK = 128
HEAD_DIM = 128
N_KV_HEADS = 4
GRID_W = 64
ROPE_THETA = 10000.0

V7X_VMEM_BYTES = 64 * 1024 * 1024
VMEM_RESERVE_BYTES = 6 * 1024 * 1024
VMEM_TEMP_BYTES = 8 * 1024 * 1024
LANES = 128

BF16 = jnp.bfloat16
F32 = jnp.float32


def _vmem_limit(estimate_bytes):
    return int(min(V7X_VMEM_BYTES - VMEM_RESERVE_BYTES, estimate_bytes + VMEM_TEMP_BYTES))


def _rmsnorm(xf, g):
    ms = jnp.mean(xf * xf, axis=-1, keepdims=True)
    return xf * lax.rsqrt(ms + NORM_EPS) * g


def _gelu(z):
    return 0.5 * z * (1.0 + lax.erf(z * math.sqrt(0.5)))


def _mm(a, b):
    return jnp.dot(a, b, preferred_element_type=F32)


class _Cast(NamedTuple):
    w: jax.Array
    layer: int
    transform: Optional[Callable] = None


def _call_with_casts(body, *, name, grid, in_specs, out_specs, out_shape, args,
                     est, casts=(), scratch_shapes=()):
    n_in, n_out, n_cast = len(in_specs), len(out_specs), len(casts)
    n_steps = math.prod(grid)

    def step(*idx):
        lin = idx[0]
        for i, n in zip(idx[1:], grid[1:]):
            lin = lin * n + i
        return lin

    cast_layers = [c.layer for c in casts]
    cast_shapes = [c.w.shape[1:] for c in casts]
    cast_fns = [c.transform for c in casts]
    casts = [c.w.reshape(-1, c.w.shape[2]) for c in casts]

    def kernel(*refs):
        ins, refs = refs[:n_in], refs[n_in:]
        cast_in, refs = refs[:n_cast], refs[n_cast:]
        outs, refs = refs[:n_out], refs[n_out:]
        cast_out, scratch = refs[:n_cast], refs[n_cast:]
        body(*ins, *outs, *scratch)
        for src, dst, fn in zip(cast_in, cast_out, cast_fns):
            if fn is None:
                dst[...] = src[...].astype(BF16)
            else:
                fn(src, dst)

    cast_in_specs = [
        pl.BlockSpec((r // n_steps, c),
                     functools.partial(lambda *idx, layer: (layer * n_steps + step(*idx), 0),
                                       layer=layer))
        for (r, c), layer in zip(cast_shapes, cast_layers)]
    cast_out_specs = [pl.BlockSpec((r // n_steps, c), lambda *idx: (step(*idx), 0))
                      for r, c in cast_shapes]
    est += sum(2 * (r // n_steps) * c * (4 + 2) for r, c in cast_shapes)
    outs = pl.pallas_call(
        kernel,
        grid=grid,
        in_specs=list(in_specs) + cast_in_specs,
        out_specs=list(out_specs) + cast_out_specs,
        out_shape=list(out_shape) + [jax.ShapeDtypeStruct(s, BF16) for s in cast_shapes],
        scratch_shapes=list(scratch_shapes),
        compiler_params=pltpu.CompilerParams(
            dimension_semantics=("parallel",) * len(grid), vmem_limit_bytes=_vmem_limit(est)),
        name=name,
    )(*args, *casts)
    return outs[:n_out], outs[n_out:]


def _gmlp_in_kernel(x_ref, g_ref, w_ref, lg_ref, lb_ref, u_ref, v_ref, *, width):
    h = _rmsnorm(x_ref[...], g_ref[...]).astype(BF16)
    v = _gelu(_mm(h, w_ref[:, width:]))
    mu = jnp.mean(v, axis=-1, keepdims=True)
    vc = v - mu
    var = jnp.mean(vc * vc, axis=-1, keepdims=True)
    vn = vc * lax.rsqrt(var + NORM_EPS) * lg_ref[...] + lb_ref[...]
    v_ref[...] = vn.astype(BF16)
    u_ref[...] = _gelu(_mm(h, w_ref[:, :width])).astype(BF16)


def _gmlp_in(x2, g, w_in, ln_g, ln_b, *, bm, casts):
    m, d = x2.shape
    width = w_in.shape[1] // 2
    row = lambda i: (i, 0)
    fixed = lambda i: (0, 0)
    est = (2 * bm * d * 4 + w_in.size * 2 + 2 * 2 * bm * width * 2
           + bm * d * 2 + 3 * bm * width * 4)
    return _call_with_casts(
        functools.partial(_gmlp_in_kernel, width=width),
        name="gmlp_in", grid=(m // bm,),
        in_specs=[
            pl.BlockSpec((bm, d), row),
            pl.BlockSpec((1, d), fixed),
            pl.BlockSpec(w_in.shape, fixed),
            pl.BlockSpec((1, width), fixed),
            pl.BlockSpec((1, width), fixed),
        ],
        out_specs=[pl.BlockSpec((bm, width), row), pl.BlockSpec((bm, width), row)],
        out_shape=[jax.ShapeDtypeStruct((m, width), BF16)] * 2,
        args=(x2, g, w_in, ln_g, ln_b), est=est, casts=casts)


def _gmlp_out_kernel(u_ref, v_ref, ws_ref, bs_ref, wout_ref, x_ref, o_ref, y_ref, *, groups):
    bm = u_ref.shape[0]
    n_chunks = bm // CHUNK
    for g in range(groups):
        cols = slice(g * LANES, (g + 1) * LANES)
        rhs = jnp.concatenate(
            [v_ref[c * CHUNK:(c + 1) * CHUNK, cols] for c in range(n_chunks)], axis=1)
        sg = _mm(ws_ref[g], rhs)
        bias = bs_ref[:, cols]
        for c in range(n_chunks):
            rows = slice(c * CHUNK, (c + 1) * CHUNK)
            s = sg[:, c * LANES:(c + 1) * LANES] + bias
            y_ref[rows, cols] = (u_ref[rows, cols].astype(F32) * s).astype(BF16)
    o_ref[...] = x_ref[...] + _mm(y_ref[...], wout_ref[...])


def _gmlp_out(u, v, w_s, b_full, w_out, x2, *, bm, casts):
    m, width = u.shape
    d = w_out.shape[1]
    groups = w_s.shape[0]
    row = lambda i: (i, 0)
    est = (2 * 2 * bm * width * 2 + w_s.size * 2 + b_full.size * 4 + w_out.size * 2
           + 2 * 2 * bm * d * 4 + bm * width * 2 + bm * d * 4)
    return _call_with_casts(
        functools.partial(_gmlp_out_kernel, groups=groups),
        name="gmlp_out", grid=(m // bm,),
        in_specs=[
            pl.BlockSpec((bm, width), row),
            pl.BlockSpec((bm, width), row),
            pl.BlockSpec(w_s.shape, lambda i: (0, 0, 0)),
            pl.BlockSpec(b_full.shape, lambda i: (0, 0)),
            pl.BlockSpec(w_out.shape, lambda i: (0, 0)),
            pl.BlockSpec((bm, d), row),
        ],
        out_specs=[pl.BlockSpec((bm, d), row)],
        out_shape=[jax.ShapeDtypeStruct((m, d), F32)],
        scratch_shapes=[pltpu.VMEM((bm, width), BF16)],
        args=(u, v, w_s, b_full, w_out, x2), est=est, casts=casts)


def _ffn_kernel(x_ref, g_ref, w1_ref, w2_ref, gf_ref, o_ref, h_ref, *, final_norm):
    j = pl.program_id(1)

    def chunk(h, acc):
        a = jnp.maximum(_mm(h, w1_ref[...]), 0.0)
        o_ref[...] = acc + _mm((a * a).astype(BF16), w2_ref[...])

    @pl.when(j == 0)
    def _():
        xf = x_ref[...]
        h_ref[...] = _rmsnorm(xf, g_ref[...]).astype(BF16)
        chunk(h_ref[...], xf)

    @pl.when(j > 0)
    def _():
        chunk(h_ref[...], o_ref[...])

    if final_norm:
        @pl.when(j == pl.num_programs(1) - 1)
        def _():
            o_ref[...] = _rmsnorm(o_ref[...], gf_ref[...])


def _ffn(x2, g, w1, w2, g_final, *, bm, fc, final_norm):
    m, d = x2.shape
    d_ff = w1.shape[1]
    est = (2 * bm * d * 4 + 2 * d * fc * 2 + 2 * fc * d * 2 + 2 * bm * d * 4
           + bm * d * 2 + bm * fc * 4 + bm * fc * 2 + bm * d * 4)
    return pl.pallas_call(
        functools.partial(_ffn_kernel, final_norm=final_norm),
        grid=(m // bm, d_ff // fc),
        in_specs=[
            pl.BlockSpec((bm, d), lambda i, j: (i, 0)),
            pl.BlockSpec((1, d), lambda i, j: (0, 0)),
            pl.BlockSpec((d, fc), lambda i, j: (0, j)),
            pl.BlockSpec((fc, d), lambda i, j: (j, 0)),
            pl.BlockSpec((1, d), lambda i, j: (0, 0)),
        ],
        out_specs=pl.BlockSpec((bm, d), lambda i, j: (i, 0)),
        out_shape=jax.ShapeDtypeStruct((m, d), F32),
        scratch_shapes=[pltpu.VMEM((bm, d), BF16)],
        compiler_params=pltpu.CompilerParams(
            dimension_semantics=("parallel", "arbitrary"),
            vmem_limit_bytes=_vmem_limit(est)),
        name="ffn_final" if final_norm else "ffn",
    )(x2, g, w1, w2, g_final)


QUARTER = HEAD_DIM // 4


def _swap_mid_quarters(t):
    lane = lax.broadcasted_iota(jnp.int32, t.shape, 1)
    from_right = (lane >= QUARTER) & (lane < 2 * QUARTER)
    from_left = (lane >= 2 * QUARTER) & (lane < 3 * QUARTER)
    return jnp.where(from_right, pltpu.roll(t, HEAD_DIM - QUARTER, axis=1),
                     jnp.where(from_left, pltpu.roll(t, QUARTER, axis=1), t))


def _cast_qkv_weights(src_ref, dst_ref, *, qk_width):
    for c0 in range(0, qk_width, HEAD_DIM):
        cols = slice(c0, c0 + HEAD_DIM)
        dst_ref[:, cols] = _swap_mid_quarters(src_ref[:, cols]).astype(BF16)
    dst_ref[:, qk_width:] = src_ref[:, qk_width:].astype(BF16)


def _qkv_kernel(x_ref, g_ref, w_ref, qn_ref, kn_ref, cos_ref, sin_ref,
                q_ref, k_ref, v_ref, *, n_heads, n_kv, scale):
    h = _rmsnorm(x_ref[...], g_ref[...]).astype(BF16)
    cos = cos_ref[...]
    sin_signed = sin_ref[...]
    q_w = n_heads * HEAD_DIM
    kv_w = n_kv * HEAD_DIM

    def partner(t):
        return pltpu.roll(t, HEAD_DIM // 2, axis=1)

    def gained_tables(gain):
        gain_b = _swap_mid_quarters(jnp.broadcast_to(gain, cos.shape))
        return gain_b * cos, partner(gain_b) * sin_signed

    def head_group(out_ref, w_col0, out_col0, tables, out_scale):
        gc, gs = tables
        t = _mm(h, w_ref[:, w_col0:w_col0 + kv_w])
        for hd in range(n_kv):
            th = t[:, hd * HEAD_DIM:(hd + 1) * HEAD_DIM]
            r = lax.rsqrt(jnp.mean(th * th, axis=-1, keepdims=True) + NORM_EPS)
            if out_scale != 1.0:
                r = r * out_scale
            th = (th * gc + partner(th) * gs) * r
            out_ref[:, out_col0 + hd * HEAD_DIM:out_col0 + (hd + 1) * HEAD_DIM] = th.astype(BF16)

    head_group(k_ref, q_w, 0, gained_tables(kn_ref[...]), 1.0)
    q_tables = gained_tables(qn_ref[...])
    for c0 in range(0, q_w, kv_w):
        head_group(q_ref, c0, c0, q_tables, scale)
    v_ref[...] = _mm(h, w_ref[:, q_w + kv_w:]).astype(BF16)


def _qkv(x2, g, w_qkv, q_norm, k_norm, cos, sin_signed, *, bm, seq, casts):
    m, d = x2.shape
    n_kv = N_KV_HEADS
    kv_w = n_kv * HEAD_DIM
    q_w = w_qkv.shape[1] - 2 * kv_w
    n_heads = q_w // HEAD_DIM
    row = lambda i: (i, 0)
    fixed = lambda i: (0, 0)
    pos = lambda i: (i % (seq // bm), 0)
    est = (2 * bm * d * 4 + w_qkv.size * 2 + 4 * bm * HEAD_DIM * 4
           + 2 * bm * (q_w + 2 * kv_w) * 2 + bm * d * 2 + bm * q_w * 4 * 2)
    return _call_with_casts(
        functools.partial(_qkv_kernel, n_heads=n_heads, n_kv=n_kv,
                          scale=HEAD_DIM ** -0.5 * math.log2(math.e)),
        name="attn_qkv", grid=(m // bm,),
        in_specs=[
            pl.BlockSpec((bm, d), row),
            pl.BlockSpec((1, d), fixed),
            pl.BlockSpec(w_qkv.shape, fixed),
            pl.BlockSpec((1, HEAD_DIM), fixed),
            pl.BlockSpec((1, HEAD_DIM), fixed),
            pl.BlockSpec((bm, HEAD_DIM), pos),
            pl.BlockSpec((bm, HEAD_DIM), pos),
        ],
        out_specs=[pl.BlockSpec((bm, q_w), row), pl.BlockSpec((bm, kv_w), row),
                   pl.BlockSpec((bm, kv_w), row)],
        out_shape=[jax.ShapeDtypeStruct((m, q_w), BF16),
                   jax.ShapeDtypeStruct((m, kv_w), BF16),
                   jax.ShapeDtypeStruct((m, kv_w), BF16)],
        args=(x2, g, w_qkv, q_norm, k_norm, cos, sin_signed), est=est, casts=casts)


def _attn_kernel(q_ref, k_ref, v_ref, o_ref, *, n_rep, rows):
    qb = q_ref.shape[0]
    v = v_ref[...]
    v_ones = jnp.concatenate([v, jnp.ones_like(v)], axis=1)
    for r in range(n_rep):
        cols = slice(r * HEAD_DIM, (r + 1) * HEAD_DIM)
        for r0 in range(0, qb, rows):
            rs = slice(r0, r0 + rows)
            s = lax.dot_general(q_ref[rs, cols], k_ref[...], (((1,), (1,)), ((), ())),
                                preferred_element_type=F32)
            p = jnp.exp2(s - jnp.max(s, axis=-1, keepdims=True))
            pv = _mm(p.astype(BF16), v_ones)
            o_ref[rs, cols] = (pv[:, :HEAD_DIM] / pv[:, HEAD_DIM:]).astype(BF16)


def _attention(q, k, v, *, batch, seq, qb, rows, casts):
    m, q_w = q.shape
    n_kv = k.shape[1] // HEAD_DIM
    n_rep = q_w // HEAD_DIM // n_kv
    nb = seq // qb
    est = (2 * 2 * qb * n_rep * HEAD_DIM * 2 + 2 * 2 * seq * HEAD_DIM * 2
           + seq * 2 * HEAD_DIM * 2 + 4 * rows * seq * (4 + 2))
    return _call_with_casts(
        functools.partial(_attn_kernel, n_rep=n_rep, rows=rows),
        name="attn_core", grid=(batch, n_kv, nb),
        in_specs=[
            pl.BlockSpec((qb, n_rep * HEAD_DIM), lambda b, g, i: (b * nb + i, g)),
            pl.BlockSpec((seq, HEAD_DIM), lambda b, g, i: (b, g)),
            pl.BlockSpec((seq, HEAD_DIM), lambda b, g, i: (b, g)),
        ],
        out_specs=[pl.BlockSpec((qb, n_rep * HEAD_DIM), lambda b, g, i: (b * nb + i, g))],
        out_shape=[jax.ShapeDtypeStruct((m, q_w), BF16)],
        args=(q, k, v), est=est, casts=casts)


def _proj_kernel(a_ref, w_ref, x_ref, o_ref):
    o_ref[...] = x_ref[...] + _mm(a_ref[...], w_ref[...])


def _proj_residual(a, w, x2, *, bm):
    m, kdim = a.shape
    d = w.shape[1]
    row = lambda i: (i, 0)
    est = 2 * bm * kdim * 2 + w.size * 2 + 2 * 2 * bm * d * 4 + bm * d * 4
    return pl.pallas_call(
        _proj_kernel,
        grid=(m // bm,),
        in_specs=[pl.BlockSpec((bm, kdim), row), pl.BlockSpec(w.shape, lambda i: (0, 0)),
                  pl.BlockSpec((bm, d), row)],
        out_specs=pl.BlockSpec((bm, d), row),
        out_shape=jax.ShapeDtypeStruct((m, d), F32),
        compiler_params=pltpu.CompilerParams(
            dimension_semantics=("parallel",), vmem_limit_bytes=_vmem_limit(est)),
        name="attn_out",
    )(a, w, x2)


def _rope_tables(seq):
    axis_dim = HEAD_DIM // 2
    t = np.arange(seq)
    inv_freq = ROPE_THETA ** (-np.arange(0, axis_dim, 2, dtype=np.float64) / axis_dim)
    ang_r = (t // GRID_W)[:, None] * inv_freq[None, :]
    ang_c = (t % GRID_W)[:, None] * inv_freq[None, :]
    cr, sr, cc, sc = np.cos(ang_r), np.sin(ang_r), np.cos(ang_c), np.sin(ang_c)
    assert cr.shape == (seq, QUARTER)
    return (jnp.asarray(np.concatenate([cr, cc, cr, cc], axis=-1), dtype=F32),
            jnp.asarray(np.concatenate([-sr, -sc, sr, sc], axis=-1), dtype=F32))


def kernel(x, gm_w_in, gm_ln_g, gm_ln_b, gm_w_s, gm_b_s, gm_w_out, attn_w_qkv, attn_q_norm, attn_k_norm, attn_w_o, ffn_w1, ffn_w2, norm_mix, norm_ffn, norm_final):
    batch, seq, d = x.shape
    assert norm_mix.shape[0] == 2, "two layers: gMLP mixer then attention mixer"
    m = batch * seq
    width = gm_w_out.shape[1]
    x2 = x.reshape(m, d)
    cos, sin_signed = _rope_tables(seq)
    bm = 512
    fc = 2048

    (u, v), (w_out, w1_0, w2_0) = _gmlp_in(
        x2, norm_mix[0].reshape(1, d), gm_w_in[0].astype(BF16),
        gm_ln_g[0].reshape(1, width), gm_ln_b[0].reshape(1, width), bm=bm,
        casts=(_Cast(gm_w_out, 0), _Cast(ffn_w1, 0), _Cast(ffn_w2, 0)))
    b_full = jnp.repeat(gm_b_s[0].T, width // gm_b_s.shape[1], axis=1)
    (x2,), (w_qkv,) = _gmlp_out(
        u, v, gm_w_s[0].astype(BF16), b_full, w_out, x2, bm=bm,
        casts=(_Cast(attn_w_qkv, 0, functools.partial(
            _cast_qkv_weights, qk_width=attn_w_qkv.shape[2] - N_KV_HEADS * HEAD_DIM)),))
    x2 = _ffn(x2, norm_ffn[0].reshape(1, d), w1_0, w2_0, norm_final.reshape(1, d),
              bm=bm, fc=fc, final_norm=False)

    (q, k, v), (w_o,) = _qkv(
        x2, norm_mix[1].reshape(1, d), w_qkv, attn_q_norm[0].reshape(1, HEAD_DIM),
        attn_k_norm[0].reshape(1, HEAD_DIM), cos, sin_signed, bm=bm, seq=seq,
        casts=(_Cast(attn_w_o, 0),))
    (o,), (w1_1, w2_1) = _attention(q, k, v, batch=batch, seq=seq, qb=2048, rows=128,
                                    casts=(_Cast(ffn_w1, 1), _Cast(ffn_w2, 1)))
    x2 = _proj_residual(o, w_o, x2, bm=bm)
    x2 = _ffn(x2, norm_ffn[1].reshape(1, d), w1_1, w2_1, norm_final.reshape(1, d),
              bm=bm, fc=fc, final_norm=True)
    return x2.reshape(batch, seq, d)
```

```python
import functools
import math
from typing import Callable, NamedTuple, Optional

import jax
import jax.numpy as jnp
import numpy as np
from jax import lax
from jax.experimental import pallas as pl
from jax.experimental.pallas import tpu as pltpu

NORM_EPS = 1e-6
CHUNK = 128
HEAD_DIM = 128
N_KV_HEADS = 4
GRID_W = 64
ROPE_THETA = 10000.0

V7X_VMEM_BYTES = 64 * 1024 * 1024
VMEM_RESERVE_BYTES = 6 * 1024 * 1024
VMEM_TEMP_BYTES = 8 * 1024 * 1024
LANES = 128

BF16 = jnp.bfloat16
F32 = jnp.float32


def _vmem_limit(estimate_bytes):
    return int(min(V7X_VMEM_BYTES - VMEM_RESERVE_BYTES, estimate_bytes + VMEM_TEMP_BYTES))


def _rmsnorm(xf, g):
    ms = jnp.mean(xf * xf, axis=-1, keepdims=True)
    return xf * lax.rsqrt(ms + NORM_EPS) * g


def _gelu(z):
    return 0.5 * z * (1.0 + lax.erf(z * math.sqrt(0.5)))


def _mm(a, b):
    return jnp.dot(a, b, preferred_element_type=F32)


class _Cast(NamedTuple):
    w: jax.Array
    layer: int
    transform: Optional[Callable] = None


def _call_with_casts(body, *, name, grid, in_specs, out_specs, out_shape, args,
                     est, casts=(), scratch_shapes=()):
    n_in, n_out, n_cast = len(in_specs), len(out_specs), len(casts)
    n_steps = math.prod(grid)

    def step(*idx):
        lin = idx[0]
        for i, n in zip(idx[1:], grid[1:]):
            lin = lin * n + i
        return lin

    cast_layers = [c.layer for c in casts]
    cast_shapes = [c.w.shape[1:] for c in casts]
    cast_fns = [c.transform for c in casts]
    casts = [c.w.reshape(-1, c.w.shape[2]) for c in casts]

    def kernel(*refs):
        ins, refs = refs[:n_in], refs[n_in:]
        cast_in, refs = refs[:n_cast], refs[n_cast:]
        outs, refs = refs[:n_out], refs[n_out:]
        cast_out, scratch = refs[:n_cast], refs[n_cast:]
        body(*ins, *outs, *scratch)
        for src, dst, fn in zip(cast_in, cast_out, cast_fns):
            if fn is None:
                dst[...] = src[...].astype(BF16)
            else:
                fn(src, dst)

    cast_in_specs = [
        pl.BlockSpec((r // n_steps, c),
                     functools.partial(lambda *idx, layer: (layer * n_steps + step(*idx), 0),
                                       layer=layer))
        for (r, c), layer in zip(cast_shapes, cast_layers)]
    cast_out_specs = [pl.BlockSpec((r // n_steps, c), lambda *idx: (step(*idx), 0))
                      for r, c in cast_shapes]
    est += sum(2 * (r // n_steps) * c * (4 + 2) for r, c in cast_shapes)
    outs = pl.pallas_call(
        kernel,
        grid=grid,
        in_specs=list(in_specs) + cast_in_specs,
        out_specs=list(out_specs) + cast_out_specs,
        out_shape=list(out_shape) + [jax.ShapeDtypeStruct(s, BF16) for s in cast_shapes],
        scratch_shapes=list(scratch_shapes),
        compiler_params=pltpu.CompilerParams(
            dimension_semantics=("parallel",) * len(grid), vmem_limit_bytes=_vmem_limit(est)),
        name=name,
    )(*args, *casts)
    return outs[:n_out], outs[n_out:]


def _gmlp_in_kernel(x_ref, g_ref, w_ref, lg_ref, lb_ref, u_ref, v_ref, *, width):
    h = _rmsnorm(x_ref[...], g_ref[...]).astype(BF16)
    v = _gelu(_mm(h, w_ref[:, width:]))
    mu = jnp.mean(v, axis=-1, keepdims=True)
    vc = v - mu
    var = jnp.mean(vc * vc, axis=-1, keepdims=True)
    vn = vc * lax.rsqrt(var + NORM_EPS) * lg_ref[...] + lb_ref[...]
    v_ref[...] = vn.astype(BF16)
    u_ref[...] = _gelu(_mm(h, w_ref[:, :width])).astype(BF16)


def _gmlp_in(x2, g, w_in, ln_g, ln_b, *, bm, casts):
    m, d = x2.shape
    width = w_in.shape[1] // 2
    row = lambda i: (i, 0)
    fixed = lambda i: (0, 0)
    est = (2 * bm * d * 4 + w_in.size * 2 + 2 * 2 * bm * width * 2
           + bm * d * 2 + 3 * bm * width * 4)
    return _call_with_casts(
        functools.partial(_gmlp_in_kernel, width=width),
        name="gmlp_in", grid=(m // bm,),
        in_specs=[
            pl.BlockSpec((bm, d), row),
            pl.BlockSpec((1, d), fixed),
            pl.BlockSpec(w_in.shape, fixed),
            pl.BlockSpec((1, width), fixed),
            pl.BlockSpec((1, width), fixed),
        ],
        out_specs=[pl.BlockSpec((bm, width), row), pl.BlockSpec((bm, width), row)],
        out_shape=[jax.ShapeDtypeStruct((m, width), BF16)] * 2,
        args=(x2, g, w_in, ln_g, ln_b), est=est, casts=casts)


def _gmlp_out_kernel(u_ref, v_ref, ws_ref, bs_ref, wout_ref, x_ref, o_ref, y_ref, *, groups):
    bm = u_ref.shape[0]
    n_chunks = bm // CHUNK
    for g in range(groups):
        cols = slice(g * LANES, (g + 1) * LANES)
        rhs = jnp.concatenate(
            [v_ref[c * CHUNK:(c + 1) * CHUNK, cols] for c in range(n_chunks)], axis=1)
        sg = _mm(ws_ref[g], rhs)
        bias = bs_ref[:, cols]
        for c in range(n_chunks):
            rows = slice(c * CHUNK, (c + 1) * CHUNK)
            s = sg[:, c * LANES:(c + 1) * LANES] + bias
            y_ref[rows, cols] = (u_ref[rows, cols].astype(F32) * s).astype(BF16)
    o_ref[...] = x_ref[...] + _mm(y_ref[...], wout_ref[...])


def _gmlp_out(u, v, w_s, b_full, w_out, x2, *, bm, casts):
    m, width = u.shape
    d = w_out.shape[1]
    groups = w_s.shape[0]
    row = lambda i: (i, 0)
    est = (2 * 2 * bm * width * 2 + w_s.size * 2 + b_full.size * 4 + w_out.size * 2
           + 2 * 2 * bm * d * 4 + bm * width * 2 + bm * d * 4)
    return _call_with_casts(
        functools.partial(_gmlp_out_kernel, groups=groups),
        name="gmlp_out", grid=(m // bm,),
        in_specs=[
            pl.BlockSpec((bm, width), row),
            pl.BlockSpec((bm, width), row),
            pl.BlockSpec(w_s.shape, lambda i: (0, 0, 0)),
            pl.BlockSpec(b_full.shape, lambda i: (0, 0)),
            pl.BlockSpec(w_out.shape, lambda i: (0, 0)),
            pl.BlockSpec((bm, d), row),
        ],
        out_specs=[pl.BlockSpec((bm, d), row)],
        out_shape=[jax.ShapeDtypeStruct((m, d), F32)],
        scratch_shapes=[pltpu.VMEM((bm, width), BF16)],
        args=(u, v, w_s, b_full, w_out, x2), est=est, casts=casts)


def _ffn_kernel(x_ref, g_ref, w1_ref, w2_ref, gf_ref, o_ref, h_ref, *, final_norm):
    j = pl.program_id(1)

    def chunk(h, acc):
        a = jnp.maximum(_mm(h, w1_ref[...]), 0.0)
        o_ref[...] = acc + _mm((a * a).astype(BF16), w2_ref[...])

    @pl.when(j == 0)
    def _():
        xf = x_ref[...]
        h_ref[...] = _rmsnorm(xf, g_ref[...]).astype(BF16)
        chunk(h_ref[...], xf)

    @pl.when(j > 0)
    def _():
        chunk(h_ref[...], o_ref[...])

    if final_norm:
        @pl.when(j == pl.num_programs(1) - 1)
        def _():
            o_ref[...] = _rmsnorm(o_ref[...], gf_ref[...])


def _ffn(x2, g, w1, w2, g_final, *, bm, fc, final_norm):
    m, d = x2.shape
    d_ff = w1.shape[1]
    est = (2 * bm * d * 4 + 2 * d * fc * 2 + 2 * fc * d * 2 + 2 * bm * d * 4
           + bm * d * 2 + bm * fc * 4 + bm * fc * 2 + bm * d * 4)
    return pl.pallas_call(
        functools.partial(_ffn_kernel, final_norm=final_norm),
        grid=(m // bm, d_ff // fc),
        in_specs=[
            pl.BlockSpec((bm, d), lambda i, j: (i, 0)),
            pl.BlockSpec((1, d), lambda i, j: (0, 0)),
            pl.BlockSpec((d, fc), lambda i, j: (0, j)),
            pl.BlockSpec((fc, d), lambda i, j: (j, 0)),
            pl.BlockSpec((1, d), lambda i, j: (0, 0)),
        ],
        out_specs=pl.BlockSpec((bm, d), lambda i, j: (i, 0)),
        out_shape=jax.ShapeDtypeStruct((m, d), F32),
        scratch_shapes=[pltpu.VMEM((bm, d), BF16)],
        compiler_params=pltpu.CompilerParams(
            dimension_semantics=("parallel", "arbitrary"),
            vmem_limit_bytes=_vmem_limit(est)),
        name="ffn_final" if final_norm else "ffn",
    )(x2, g, w1, w2, g_final)


QUARTER = HEAD_DIM // 4


def _swap_mid_quarters(t):
    lane = lax.broadcasted_iota(jnp.int32, t.shape, 1)
    from_right = (lane >= QUARTER) & (lane < 2 * QUARTER)
    from_left = (lane >= 2 * QUARTER) & (lane < 3 * QUARTER)
    return jnp.where(from_right, pltpu.roll(t, HEAD_DIM - QUARTER, axis=1),
                     jnp.where(from_left, pltpu.roll(t, QUARTER, axis=1), t))


def _cast_qkv_weights(src_ref, dst_ref, *, qk_width):
    for c0 in range(0, qk_width, HEAD_DIM):
        cols = slice(c0, c0 + HEAD_DIM)
        dst_ref[:, cols] = _swap_mid_quarters(src_ref[:, cols]).astype(BF16)
    dst_ref[:, qk_width:] = src_ref[:, qk_width:].astype(BF16)


def _qkv_kernel(x_ref, g_ref, w_ref, qn_ref, kn_ref, cos_ref, sin_ref,
                q_ref, k_ref, v_ref, *, n_heads, n_kv, scale):
    h = _rmsnorm(x_ref[...], g_ref[...]).astype(BF16)
    cos = cos_ref[...]
    sin_signed = sin_ref[...]
    q_w = n_heads * HEAD_DIM
    kv_w = n_kv * HEAD_DIM

    def partner(t):
        return pltpu.roll(t, HEAD_DIM // 2, axis=1)

    def gained_tables(gain):
        gain_b = _swap_mid_quarters(jnp.broadcast_to(gain, cos.shape))
        return gain_b * cos, partner(gain_b) * sin_signed

    def head_group(out_ref, w_col0, out_col0, tables, out_scale):
        gc, gs = tables
        t = _mm(h, w_ref[:, w_col0:w_col0 + kv_w])
        for hd in range(n_kv):
            th = t[:, hd * HEAD_DIM:(hd + 1) * HEAD_DIM]
            r = lax.rsqrt(jnp.mean(th * th, axis=-1, keepdims=True) + NORM_EPS)
            if out_scale != 1.0:
                r = r * out_scale
            th = (th * gc + partner(th) * gs) * r
            out_ref[:, out_col0 + hd * HEAD_DIM:out_col0 + (hd + 1) * HEAD_DIM] = th.astype(BF16)

    head_group(k_ref, q_w, 0, gained_tables(kn_ref[...]), 1.0)
    q_tables = gained_tables(qn_ref[...])
    for c0 in range(0, q_w, kv_w):
        head_group(q_ref, c0, c0, q_tables, scale)
    v_ref[...] = _mm(h, w_ref[:, q_w + kv_w:]).astype(BF16)


def _qkv(x2, g, w_qkv, q_norm, k_norm, cos, sin_signed, *, bm, seq, casts):
    m, d = x2.shape
    n_kv = N_KV_HEADS
    kv_w = n_kv * HEAD_DIM
    q_w = w_qkv.shape[1] - 2 * kv_w
    n_heads = q_w // HEAD_DIM
    row = lambda i: (i, 0)
    fixed = lambda i: (0, 0)
    pos = lambda i: (i % (seq // bm), 0)
    est = (2 * bm * d * 4 + w_qkv.size * 2 + 4 * bm * HEAD_DIM * 4
           + 2 * bm * (q_w + 2 * kv_w) * 2 + bm * d * 2 + bm * q_w * 4 * 2)
    return _call_with_casts(
        functools.partial(_qkv_kernel, n_heads=n_heads, n_kv=n_kv,
                          scale=HEAD_DIM ** -0.5 * math.log2(math.e)),
        name="attn_qkv", grid=(m // bm,),
        in_specs=[
            pl.BlockSpec((bm, d), row),
            pl.BlockSpec((1, d), fixed),
            pl.BlockSpec(w_qkv.shape, fixed),
            pl.BlockSpec((1, HEAD_DIM), fixed),
            pl.BlockSpec((1, HEAD_DIM), fixed),
            pl.BlockSpec((bm, HEAD_DIM), pos),
            pl.BlockSpec((bm, HEAD_DIM), pos),
        ],
        out_specs=[pl.BlockSpec((bm, q_w), row), pl.BlockSpec((bm, kv_w), row),
                   pl.BlockSpec((bm, kv_w), row)],
        out_shape=[jax.ShapeDtypeStruct((m, q_w), BF16),
                   jax.ShapeDtypeStruct((m, kv_w), BF16),
                   jax.ShapeDtypeStruct((m, kv_w), BF16)],
        args=(x2, g, w_qkv, q_norm, k_norm, cos, sin_signed), est=est, casts=casts)


def _attn_kernel(q_ref, k_ref, v_ref, o_ref, *, n_rep, rows):
    qb = q_ref.shape[0]
    v = v_ref[...]
    v_ones = jnp.concatenate([v, jnp.ones_like(v)], axis=1)
    for r in range(n_rep):
        cols = slice(r * HEAD_DIM, (r + 1) * HEAD_DIM)
        for r0 in range(0, qb, rows):
            rs = slice(r0, r0 + rows)
            s = lax.dot_general(q_ref[rs, cols], k_ref[...], (((1,), (1,)), ((), ())),
                                preferred_element_type=F32)
            p = jnp.exp2(s - jnp.max(s, axis=-1, keepdims=True))
            pv = _mm(p.astype(BF16), v_ones)
            o_ref[rs, cols] = (pv[:, :HEAD_DIM] / pv[:, HEAD_DIM:]).astype(BF16)


def _attention(q, k, v, *, batch, seq, qb, rows, casts):
    m, q_w = q.shape
    n_kv = k.shape[1] // HEAD_DIM
    n_rep = q_w // HEAD_DIM // n_kv
    nb = seq // qb
    est = (2 * 2 * qb * n_rep * HEAD_DIM * 2 + 2 * 2 * seq * HEAD_DIM * 2
           + seq * 2 * HEAD_DIM * 2 + 4 * rows * seq * (4 + 2))
    return _call_with_casts(
        functools.partial(_attn_kernel, n_rep=n_rep, rows=rows),
        name="attn_core", grid=(batch, n_kv, nb),
        in_specs=[
            pl.BlockSpec((qb, n_rep * HEAD_DIM), lambda b, g, i: (b * nb + i, g)),
            pl.BlockSpec((seq, HEAD_DIM), lambda b, g, i: (b, g)),
            pl.BlockSpec((seq, HEAD_DIM), lambda b, g, i: (b, g)),
        ],
        out_specs=[pl.BlockSpec((qb, n_rep * HEAD_DIM), lambda b, g, i: (b * nb + i, g))],
        out_shape=[jax.ShapeDtypeStruct((m, q_w), BF16)],
        args=(q, k, v), est=est, casts=casts)


def _proj_kernel(a_ref, w_ref, x_ref, o_ref):
    o_ref[...] = x_ref[...] + _mm(a_ref[...], w_ref[...])


def _proj_residual(a, w, x2, *, bm):
    m, kdim = a.shape
    d = w.shape[1]
    row = lambda i: (i, 0)
    est = 2 * bm * kdim * 2 + w.size * 2 + 2 * 2 * bm * d * 4 + bm * d * 4
    return pl.pallas_call(
        _proj_kernel,
        grid=(m // bm,),
        in_specs=[pl.BlockSpec((bm, kdim), row), pl.BlockSpec(w.shape, lambda i: (0, 0)),
                  pl.BlockSpec((bm, d), row)],
        out_specs=pl.BlockSpec((bm, d), row),
        out_shape=jax.ShapeDtypeStruct((m, d), F32),
        compiler_params=pltpu.CompilerParams(
            dimension_semantics=("parallel",), vmem_limit_bytes=_vmem_limit(est)),
        name="attn_out",
    )(a, w, x2)


def _rope_tables(seq):
    axis_dim = HEAD_DIM // 2
    t = np.arange(seq)
    inv_freq = ROPE_THETA ** (-np.arange(0, axis_dim, 2, dtype=np.float64) / axis_dim)
    ang_r = (t // GRID_W)[:, None] * inv_freq[None, :]
    ang_c = (t % GRID_W)[:, None] * inv_freq[None, :]
    cr, sr, cc, sc = np.cos(ang_r), np.sin(ang_r), np.cos(ang_c), np.sin(ang_c)
    assert cr.shape == (seq, QUARTER)
    return (jnp.asarray(np.concatenate([cr, cc, cr, cc], axis=-1), dtype=F32),
            jnp.asarray(np.concatenate([-sr, -sc, sr, sc], axis=-1), dtype=F32))


def kernel(x, gm_w_in, gm_ln_g, gm_ln_b, gm_w_s, gm_b_s, gm_w_out, attn_w_qkv, attn_q_norm, attn_k_norm, attn_w_o, ffn_w1, ffn_w2, norm_mix, norm_ffn, norm_final):
    batch, seq, d = x.shape
    assert norm_mix.shape[0] == 2, "two layers: gMLP mixer then attention mixer"
    m = batch * seq
    width = gm_w_out.shape[1]
    x2 = x.reshape(m, d)
    cos, sin_signed = _rope_tables(seq)
    bm = 512
    fc = 2048

    (u, v), (w_out, w1_0, w2_0, w_qkv) = _gmlp_in(
        x2, norm_mix[0].reshape(1, d), gm_w_in[0].astype(BF16),
        gm_ln_g[0].reshape(1, width), gm_ln_b[0].reshape(1, width), bm=bm,
        casts=(_Cast(gm_w_out, 0), _Cast(ffn_w1, 0), _Cast(ffn_w2, 0),
               _Cast(attn_w_qkv, 0, functools.partial(
                   _cast_qkv_weights, qk_width=attn_w_qkv.shape[2] - N_KV_HEADS * HEAD_DIM))))
    b_full = jnp.repeat(gm_b_s[0].T, width // gm_b_s.shape[1], axis=1)
    (x2,), _ = _gmlp_out(u, v, gm_w_s[0].astype(BF16), b_full, w_out, x2, bm=bm, casts=())
    x2 = _ffn(x2, norm_ffn[0].reshape(1, d), w1_0, w2_0, norm_final.reshape(1, d),
              bm=bm, fc=fc, final_norm=False)

    (q, k, v), _ = _qkv(
        x2, norm_mix[1].reshape(1, d), w_qkv, attn_q_norm[0].reshape(1, HEAD_DIM),
        attn_k_norm[0].reshape(1, HEAD_DIM), cos, sin_signed, bm=bm, seq=seq, casts=())
    (o,), (w_o, w1_1, w2_1) = _attention(
        q, k, v, batch=batch, seq=seq, qb=2048, rows=128,
        casts=(_Cast(attn_w_o, 0), _Cast(ffn_w1, 1), _Cast(ffn_w2, 1)))
    x2 = _proj_residual(o, w_o, x2, bm=bm)
    x2 = _ffn(x2, norm_ffn[1].reshape(1, d), w1_1, w2_1, norm_final.reshape(1, d),
              bm=bm, fc=fc, final_norm=True)
    return x2.reshape(batch, seq, d)
```

```python
import functools
import math
from typing import Callable, NamedTuple, Optional

import jax
import jax.numpy as jnp
import numpy as np
from jax import lax
from jax.experimental import pallas as pl
from jax.experimental.pallas import tpu as pltpu

NORM_EPS = 1e-6
CHUNK = 128
HEAD_DIM = 128
N_KV_HEADS = 4
GRID_W = 64
ROPE_THETA = 10000.0

V7X_VMEM_BYTES = 64 * 1024 * 1024
VMEM_RESERVE_BYTES = 6 * 1024 * 1024
VMEM_TEMP_BYTES = 8 * 1024 * 1024
LANES = 128

BF16 = jnp.bfloat16
F32 = jnp.float32


def _vmem_limit(estimate_bytes):
    return int(min(V7X_VMEM_BYTES - VMEM_RESERVE_BYTES, estimate_bytes + VMEM_TEMP_BYTES))


def _rmsnorm(xf, g):
    ms = jnp.mean(xf * xf, axis=-1, keepdims=True)
    return xf * lax.rsqrt(ms + NORM_EPS) * g


def _gelu(z):
    return 0.5 * z * (1.0 + lax.erf(z * math.sqrt(0.5)))


def _mm(a, b):
    return jnp.dot(a, b, preferred_element_type=F32)


class _Cast(NamedTuple):
    w: jax.Array
    layer: int
    transform: Optional[Callable] = None
    col_chunks: int = 1


def _call_with_casts(body, *, name, grid, in_specs, out_specs, out_shape, args,
                     est, casts=(), scratch_shapes=()):
    n_in, n_out, n_cast = len(in_specs), len(out_specs), len(casts)
    n_steps = math.prod(grid)

    def step(*idx):
        lin = idx[0]
        for i, n in zip(idx[1:], grid[1:]):
            lin = lin * n + i
        return lin

    cast_layers = [c.layer for c in casts]
    cast_shapes = [c.w.shape[1:] for c in casts]
    cast_fns = [c.transform for c in casts]
    cast_chunks = [c.col_chunks for c in casts]
    casts = [c.w.reshape(-1, c.w.shape[2]) for c in casts]

    def kernel(*refs):
        ins, refs = refs[:n_in], refs[n_in:]
        cast_in, refs = refs[:n_cast], refs[n_cast:]
        outs, refs = refs[:n_out], refs[n_out:]
        cast_out, scratch = refs[:n_cast], refs[n_cast:]
        body(*ins, *outs, *scratch)
        for src, dst, fn, nc in zip(cast_in, cast_out, cast_fns, cast_chunks):
            if fn is not None:
                fn(src, dst)
            elif nc == 1:
                dst[...] = src[...].astype(BF16)
            else:
                w = src.shape[1] // nc
                for c in range(nc):
                    dst[c] = src[:, c * w:(c + 1) * w].astype(BF16)

    cast_in_specs = [
        pl.BlockSpec((r // n_steps, c),
                     functools.partial(lambda *idx, layer: (layer * n_steps + step(*idx), 0),
                                       layer=layer))
        for (r, c), layer in zip(cast_shapes, cast_layers)]
    cast_out_specs = [
        pl.BlockSpec((r // n_steps, c), lambda *idx: (step(*idx), 0)) if nc == 1 else
        pl.BlockSpec((nc, r // n_steps, c // nc), lambda *idx: (0, step(*idx), 0))
        for (r, c), nc in zip(cast_shapes, cast_chunks)]
    cast_out_shapes = [(r, c) if nc == 1 else (nc, r, c // nc)
                       for (r, c), nc in zip(cast_shapes, cast_chunks)]
    est += sum(2 * (r // n_steps) * c * (4 + 2) for r, c in cast_shapes)
    outs = pl.pallas_call(
        kernel,
        grid=grid,
        in_specs=list(in_specs) + cast_in_specs,
        out_specs=list(out_specs) + cast_out_specs,
        out_shape=list(out_shape) + [jax.ShapeDtypeStruct(s, BF16) for s in cast_out_shapes],
        scratch_shapes=list(scratch_shapes),
        compiler_params=pltpu.CompilerParams(
            dimension_semantics=("parallel",) * len(grid), vmem_limit_bytes=_vmem_limit(est)),
        name=name,
    )(*args, *casts)
    return outs[:n_out], outs[n_out:]


def _gmlp_in_kernel(x_ref, g_ref, w_ref, lg_ref, lb_ref, u_ref, v_ref, *, width):
    h = _rmsnorm(x_ref[...], g_ref[...]).astype(BF16)
    v = _gelu(_mm(h, w_ref[:, width:]))
    mu = jnp.mean(v, axis=-1, keepdims=True)
    vc = v - mu
    var = jnp.mean(vc * vc, axis=-1, keepdims=True)
    vn = vc * lax.rsqrt(var + NORM_EPS) * lg_ref[...] + lb_ref[...]
    v_ref[...] = vn.astype(BF16)
    u_ref[...] = _gelu(_mm(h, w_ref[:, :width])).astype(BF16)


def _gmlp_in(x2, g, w_in, ln_g, ln_b, *, bm, casts):
    m, d = x2.shape
    width = w_in.shape[1] // 2
    row = lambda i: (i, 0)
    fixed = lambda i: (0, 0)
    est = (2 * bm * d * 4 + w_in.size * 2 + 2 * 2 * bm * width * 2
           + bm * d * 2 + 3 * bm * width * 4)
    return _call_with_casts(
        functools.partial(_gmlp_in_kernel, width=width),
        name="gmlp_in", grid=(m // bm,),
        in_specs=[
            pl.BlockSpec((bm, d), row),
            pl.BlockSpec((1, d), fixed),
            pl.BlockSpec(w_in.shape, fixed),
            pl.BlockSpec((1, width), fixed),
            pl.BlockSpec((1, width), fixed),
        ],
        out_specs=[pl.BlockSpec((bm, width), row), pl.BlockSpec((bm, width), row)],
        out_shape=[jax.ShapeDtypeStruct((m, width), BF16)] * 2,
        args=(x2, g, w_in, ln_g, ln_b), est=est, casts=casts)


def _gmlp_out_kernel(u_ref, v_ref, ws_ref, bs_ref, wout_ref, x_ref, o_ref, y_ref, *, groups):
    bm = u_ref.shape[0]
    n_chunks = bm // CHUNK
    for g in range(groups):
        cols = slice(g * LANES, (g + 1) * LANES)
        rhs = jnp.concatenate(
            [v_ref[c * CHUNK:(c + 1) * CHUNK, cols] for c in range(n_chunks)], axis=1)
        sg = _mm(ws_ref[g], rhs)
        bias = bs_ref[:, cols]
        for c in range(n_chunks):
            rows = slice(c * CHUNK, (c + 1) * CHUNK)
            s = sg[:, c * LANES:(c + 1) * LANES] + bias
            y_ref[rows, cols] = (u_ref[rows, cols].astype(F32) * s).astype(BF16)
    o_ref[...] = x_ref[...] + _mm(y_ref[...], wout_ref[...])


def _gmlp_out(u, v, w_s, b_full, w_out, x2, *, bm, casts):
    m, width = u.shape
    d = w_out.shape[1]
    groups = w_s.shape[0]
    row = lambda i: (i, 0)
    est = (2 * 2 * bm * width * 2 + w_s.size * 2 + b_full.size * 4 + w_out.size * 2
           + 2 * 2 * bm * d * 4 + bm * width * 2 + bm * d * 4)
    return _call_with_casts(
        functools.partial(_gmlp_out_kernel, groups=groups),
        name="gmlp_out", grid=(m // bm,),
        in_specs=[
            pl.BlockSpec((bm, width), row),
            pl.BlockSpec((bm, width), row),
            pl.BlockSpec(w_s.shape, lambda i: (0, 0, 0)),
            pl.BlockSpec(b_full.shape, lambda i: (0, 0)),
            pl.BlockSpec(w_out.shape, lambda i: (0, 0)),
            pl.BlockSpec((bm, d), row),
        ],
        out_specs=[pl.BlockSpec((bm, d), row)],
        out_shape=[jax.ShapeDtypeStruct((m, d), F32)],
        scratch_shapes=[pltpu.VMEM((bm, width), BF16)],
        args=(u, v, w_s, b_full, w_out, x2), est=est, casts=casts)


def _ffn_kernel(x_ref, g_ref, w1_ref, w2_ref, gf_ref, o_ref, h_ref, *, final_norm):
    j = pl.program_id(1)

    def chunk(h, acc):
        a = jnp.maximum(_mm(h, w1_ref[...]), 0.0)
        o_ref[...] = acc + _mm((a * a).astype(BF16), w2_ref[...])

    @pl.when(j == 0)
    def _():
        xf = x_ref[...]
        h_ref[...] = _rmsnorm(xf, g_ref[...]).astype(BF16)
        chunk(h_ref[...], xf)

    @pl.when(j > 0)
    def _():
        chunk(h_ref[...], o_ref[...])

    if final_norm:
        @pl.when(j == pl.num_programs(1) - 1)
        def _():
            o_ref[...] = _rmsnorm(o_ref[...], gf_ref[...])


def _ffn(x2, g, w1, w2, g_final, *, bm, final_norm):
    m, d = x2.shape
    n_chunks, _, fc = w1.shape
    est = (2 * bm * d * 4 + 2 * d * fc * 2 + 2 * fc * d * 2 + 2 * bm * d * 4
           + bm * d * 2 + bm * fc * 4 + bm * fc * 2 + bm * d * 4)
    return pl.pallas_call(
        functools.partial(_ffn_kernel, final_norm=final_norm),
        grid=(m // bm, n_chunks),
        in_specs=[
            pl.BlockSpec((bm, d), lambda i, j: (i, 0)),
            pl.BlockSpec((1, d), lambda i, j: (0, 0)),
            pl.BlockSpec((None, d, fc), lambda i, j: (j, 0, 0)),
            pl.BlockSpec((fc, d), lambda i, j: (j, 0)),
            pl.BlockSpec((1, d), lambda i, j: (0, 0)),
        ],
        out_specs=pl.BlockSpec((bm, d), lambda i, j: (i, 0)),
        out_shape=jax.ShapeDtypeStruct((m, d), F32),
        scratch_shapes=[pltpu.VMEM((bm, d), BF16)],
        compiler_params=pltpu.CompilerParams(
            dimension_semantics=("parallel", "arbitrary"),
            vmem_limit_bytes=_vmem_limit(est)),
        name="ffn_final" if final_norm else "ffn",
    )(x2, g, w1, w2, g_final)


QUARTER = HEAD_DIM // 4


def _swap_mid_quarters(t):
    lane = lax.broadcasted_iota(jnp.int32, t.shape, 1)
    from_right = (lane >= QUARTER) & (lane < 2 * QUARTER)
    from_left = (lane >= 2 * QUARTER) & (lane < 3 * QUARTER)
    return jnp.where(from_right, pltpu.roll(t, HEAD_DIM - QUARTER, axis=1),
                     jnp.where(from_left, pltpu.roll(t, QUARTER, axis=1), t))


def _cast_qkv_weights(src_ref, dst_ref, *, qk_width):
    for c0 in range(0, qk_width, HEAD_DIM):
        cols = slice(c0, c0 + HEAD_DIM)
        dst_ref[:, cols] = _swap_mid_quarters(src_ref[:, cols]).astype(BF16)
    dst_ref[:, qk_width:] = src_ref[:, qk_width:].astype(BF16)


def _qkv_kernel(x_ref, g_ref, w_ref, qn_ref, kn_ref, cos_ref, sin_ref,
                q_ref, k_ref, v_ref, *, n_heads, n_kv, scale):
    h = _rmsnorm(x_ref[...], g_ref[...]).astype(BF16)
    cos = cos_ref[...]
    sin_signed = sin_ref[...]
    q_w = n_heads * HEAD_DIM
    kv_w = n_kv * HEAD_DIM

    def partner(t):
        return pltpu.roll(t, HEAD_DIM // 2, axis=1)

    def gained_tables(gain):
        gain_b = _swap_mid_quarters(jnp.broadcast_to(gain, cos.shape))
        return gain_b * cos, partner(gain_b) * sin_signed

    def head_group(out_ref, w_col0, out_col0, tables, out_scale):
        gc, gs = tables
        t = _mm(h, w_ref[:, w_col0:w_col0 + kv_w])
        for hd in range(n_kv):
            th = t[:, hd * HEAD_DIM:(hd + 1) * HEAD_DIM]
            r = lax.rsqrt(jnp.mean(th * th, axis=-1, keepdims=True) + NORM_EPS)
            if out_scale != 1.0:
                r = r * out_scale
            th = (th * gc + partner(th) * gs) * r
            out_ref[:, out_col0 + hd * HEAD_DIM:out_col0 + (hd + 1) * HEAD_DIM] = th.astype(BF16)

    head_group(k_ref, q_w, 0, gained_tables(kn_ref[...]), 1.0)
    q_tables = gained_tables(qn_ref[...])
    for c0 in range(0, q_w, kv_w):
        head_group(q_ref, c0, c0, q_tables, scale)
    v_ref[...] = _mm(h, w_ref[:, q_w + kv_w:]).astype(BF16)


def _qkv(x2, g, w_qkv, q_norm, k_norm, cos, sin_signed, *, bm, seq, casts):
    m, d = x2.shape
    n_kv = N_KV_HEADS
    kv_w = n_kv * HEAD_DIM
    q_w = w_qkv.shape[1] - 2 * kv_w
    n_heads = q_w // HEAD_DIM
    row = lambda i: (i, 0)
    fixed = lambda i: (0, 0)
    pos = lambda i: (i % (seq // bm), 0)
    est = (2 * bm * d * 4 + w_qkv.size * 2 + 4 * bm * HEAD_DIM * 4
           + 2 * bm * (q_w + 2 * kv_w) * 2 + bm * d * 2 + bm * q_w * 4 * 2)
    return _call_with_casts(
        functools.partial(_qkv_kernel, n_heads=n_heads, n_kv=n_kv,
                          scale=HEAD_DIM ** -0.5 * math.log2(math.e)),
        name="attn_qkv", grid=(m // bm,),
        in_specs=[
            pl.BlockSpec((bm, d), row),
            pl.BlockSpec((1, d), fixed),
            pl.BlockSpec(w_qkv.shape, fixed),
            pl.BlockSpec((1, HEAD_DIM), fixed),
            pl.BlockSpec((1, HEAD_DIM), fixed),
            pl.BlockSpec((bm, HEAD_DIM), pos),
            pl.BlockSpec((bm, HEAD_DIM), pos),
        ],
        out_specs=[pl.BlockSpec((bm, q_w), row), pl.BlockSpec((bm, kv_w), row),
                   pl.BlockSpec((bm, kv_w), row)],
        out_shape=[jax.ShapeDtypeStruct((m, q_w), BF16),
                   jax.ShapeDtypeStruct((m, kv_w), BF16),
                   jax.ShapeDtypeStruct((m, kv_w), BF16)],
        args=(x2, g, w_qkv, q_norm, k_norm, cos, sin_signed), est=est, casts=casts)


def _attn_kernel(q_ref, k_ref, v_ref, o_ref, *, n_rep, rows):
    qb = q_ref.shape[0]
    v = v_ref[...]
    v_ones = jnp.concatenate([v, jnp.ones_like(v)], axis=1)
    for r in range(n_rep):
        cols = slice(r * HEAD_DIM, (r + 1) * HEAD_DIM)
        for r0 in range(0, qb, rows):
            rs = slice(r0, r0 + rows)
            s = lax.dot_general(q_ref[rs, cols], k_ref[...], (((1,), (1,)), ((), ())),
                                preferred_element_type=F32)
            p = jnp.exp2(s - jnp.max(s, axis=-1, keepdims=True))
            pv = _mm(p.astype(BF16), v_ones)
            o_ref[rs, cols] = (pv[:, :HEAD_DIM] / pv[:, HEAD_DIM:]).astype(BF16)


def _attention(q, k, v, *, batch, seq, qb, rows, casts):
    m, q_w = q.shape
    n_kv = k.shape[1] // HEAD_DIM
    n_rep = q_w // HEAD_DIM // n_kv
    nb = seq // qb
    est = (2 * 2 * qb * n_rep * HEAD_DIM * 2 + 2 * 2 * seq * HEAD_DIM * 2
           + seq * 2 * HEAD_DIM * 2 + 4 * rows * seq * (4 + 2))
    return _call_with_casts(
        functools.partial(_attn_kernel, n_rep=n_rep, rows=rows),
        name="attn_core", grid=(batch, n_kv, nb),
        in_specs=[
            pl.BlockSpec((qb, n_rep * HEAD_DIM), lambda b, g, i: (b * nb + i, g)),
            pl.BlockSpec((seq, HEAD_DIM), lambda b, g, i: (b, g)),
            pl.BlockSpec((seq, HEAD_DIM), lambda b, g, i: (b, g)),
        ],
        out_specs=[pl.BlockSpec((qb, n_rep * HEAD_DIM), lambda b, g, i: (b * nb + i, g))],
        out_shape=[jax.ShapeDtypeStruct((m, q_w), BF16)],
        args=(q, k, v), est=est, casts=casts)


def _proj_kernel(a_ref, w_ref, x_ref, o_ref):
    o_ref[...] = x_ref[...] + _mm(a_ref[...], w_ref[...])


def _proj_residual(a, w, x2, *, bm):
    m, kdim = a.shape
    d = w.shape[1]
    row = lambda i: (i, 0)
    est = 2 * bm * kdim * 2 + w.size * 2 + 2 * 2 * bm * d * 4 + bm * d * 4
    return pl.pallas_call(
        _proj_kernel,
        grid=(m // bm,),
        in_specs=[pl.BlockSpec((bm, kdim), row), pl.BlockSpec(w.shape, lambda i: (0, 0)),
                  pl.BlockSpec((bm, d), row)],
        out_specs=pl.BlockSpec((bm, d), row),
        out_shape=jax.ShapeDtypeStruct((m, d), F32),
        compiler_params=pltpu.CompilerParams(
            dimension_semantics=("parallel",), vmem_limit_bytes=_vmem_limit(est)),
        name="attn_out",
    )(a, w, x2)


def _rope_tables(seq):
    axis_dim = HEAD_DIM // 2
    t = np.arange(seq)
    inv_freq = ROPE_THETA ** (-np.arange(0, axis_dim, 2, dtype=np.float64) / axis_dim)
    ang_r = (t // GRID_W)[:, None] * inv_freq[None, :]
    ang_c = (t % GRID_W)[:, None] * inv_freq[None, :]
    cr, sr, cc, sc = np.cos(ang_r), np.sin(ang_r), np.cos(ang_c), np.sin(ang_c)
    assert cr.shape == (seq, QUARTER)
    return (jnp.asarray(np.concatenate([cr, cc, cr, cc], axis=-1), dtype=F32),
            jnp.asarray(np.concatenate([-sr, -sc, sr, sc], axis=-1), dtype=F32))


def kernel(x, gm_w_in, gm_ln_g, gm_ln_b, gm_w_s, gm_b_s, gm_w_out, attn_w_qkv, attn_q_norm, attn_k_norm, attn_w_o, ffn_w1, ffn_w2, norm_mix, norm_ffn, norm_final):
    batch, seq, d = x.shape
    assert norm_mix.shape[0] == 2, "two layers: gMLP mixer then attention mixer"
    m = batch * seq
    width = gm_w_out.shape[1]
    x2 = x.reshape(m, d)
    cos, sin_signed = _rope_tables(seq)
    bm = 512
    ffn_chunks = ffn_w1.shape[2] // 2048

    (u, v), (w_out, w1_0, w2_0, w_qkv) = _gmlp_in(
        x2, norm_mix[0].reshape(1, d), gm_w_in[0].astype(BF16),
        gm_ln_g[0].reshape(1, width), gm_ln_b[0].reshape(1, width), bm=bm,
        casts=(_Cast(gm_w_out, 0), _Cast(ffn_w1, 0, col_chunks=ffn_chunks), _Cast(ffn_w2, 0),
               _Cast(attn_w_qkv, 0, functools.partial(
                   _cast_qkv_weights, qk_width=attn_w_qkv.shape[2] - N_KV_HEADS * HEAD_DIM))))
    b_full = jnp.repeat(gm_b_s[0].T, width // gm_b_s.shape[1], axis=1)
    (x2,), _ = _gmlp_out(u, v, gm_w_s[0].astype(BF16), b_full, w_out, x2, bm=bm, casts=())
    x2 = _ffn(x2, norm_ffn[0].reshape(1, d), w1_0, w2_0, norm_final.reshape(1, d),
              bm=bm, final_norm=False)

    (q, k, v), _ = _qkv(
        x2, norm_mix[1].reshape(1, d), w_qkv, attn_q_norm[0].reshape(1, HEAD_DIM),
        attn_k_norm[0].reshape(1, HEAD_DIM), cos, sin_signed, bm=bm, seq=seq, casts=())
    (o,), (w_o, w1_1, w2_1) = _attention(
        q, k, v, batch=batch, seq=seq, qb=2048, rows=128,
        casts=(_Cast(attn_w_o, 0), _Cast(ffn_w1, 1, col_chunks=ffn_chunks), _Cast(ffn_w2, 1)))
    x2 = _proj_residual(o, w_o, x2, bm=bm)
    x2 = _ffn(x2, norm_ffn[1].reshape(1, d), w1_1, w2_1, norm_final.reshape(1, d),
              bm=bm, final_norm=True)
    return x2.reshape(batch, seq, d)
```

```python
import functools
import math
from typing import Callable, NamedTuple, Optional

import jax
import jax.numpy as jnp
import numpy as np
from jax import lax
from jax.experimental import pallas as pl
from jax.experimental.pallas import tpu as pltpu

NORM_EPS = 1e-6
CHUNK = 128
HEAD_DIM = 128
N_KV_HEADS = 4
GRID_W = 64
ROPE_THETA = 10000.0

V7X_VMEM_BYTES = 64 * 1024 * 1024
VMEM_RESERVE_BYTES = 6 * 1024 * 1024
VMEM_TEMP_BYTES = 8 * 1024 * 1024
LANES = 128

BF16 = jnp.bfloat16
F32 = jnp.float32


def _vmem_limit(estimate_bytes):
    return int(min(V7X_VMEM_BYTES - VMEM_RESERVE_BYTES, estimate_bytes + VMEM_TEMP_BYTES))


def _rmsnorm(xf, g):
    ms = jnp.mean(xf * xf, axis=-1, keepdims=True)
    return xf * lax.rsqrt(ms + NORM_EPS) * g


def _gelu(z):
    return 0.5 * z * (1.0 + lax.erf(z * math.sqrt(0.5)))


def _mm(a, b):
    return jnp.dot(a, b, preferred_element_type=F32)


class _Cast(NamedTuple):
    w: jax.Array
    layer: int
    transform: Optional[Callable] = None


def _call_with_casts(body, *, name, grid, in_specs, out_specs, out_shape, args,
                     est, casts=(), scratch_shapes=()):
    n_in, n_out, n_cast = len(in_specs), len(out_specs), len(casts)
    n_steps = math.prod(grid)

    def step(*idx):
        lin = idx[0]
        for i, n in zip(idx[1:], grid[1:]):
            lin = lin * n + i
        return lin

    cast_layers = [c.layer for c in casts]
    cast_shapes = [c.w.shape[1:] for c in casts]
    cast_fns = [c.transform for c in casts]
    casts = [c.w.reshape(-1, c.w.shape[2]) for c in casts]

    def kernel(*refs):
        ins, refs = refs[:n_in], refs[n_in:]
        cast_in, refs = refs[:n_cast], refs[n_cast:]
        outs, refs = refs[:n_out], refs[n_out:]
        cast_out, scratch = refs[:n_cast], refs[n_cast:]
        body(*ins, *outs, *scratch)
        for src, dst, fn in zip(cast_in, cast_out, cast_fns):
            if fn is None:
                dst[...] = src[...].astype(BF16)
            else:
                fn(src, dst)

    cast_in_specs = [
        pl.BlockSpec((r // n_steps, c),
                     functools.partial(lambda *idx, layer: (layer * n_steps + step(*idx), 0),
                                       layer=layer))
        for (r, c), layer in zip(cast_shapes, cast_layers)]
    cast_out_specs = [pl.BlockSpec((r // n_steps, c), lambda *idx: (step(*idx), 0))
                      for r, c in cast_shapes]
    est += sum(2 * (r // n_steps) * c * (4 + 2) for r, c in cast_shapes)
    outs = pl.pallas_call(
        kernel,
        grid=grid,
        in_specs=list(in_specs) + cast_in_specs,
        out_specs=list(out_specs) + cast_out_specs,
        out_shape=list(out_shape) + [jax.ShapeDtypeStruct(s, BF16) for s in cast_shapes],
        scratch_shapes=list(scratch_shapes),
        compiler_params=pltpu.CompilerParams(
            dimension_semantics=("parallel",) * len(grid), vmem_limit_bytes=_vmem_limit(est)),
        name=name,
    )(*args, *casts)
    return outs[:n_out], outs[n_out:]


def _gmlp_in_kernel(x_ref, g_ref, w_ref, lg_ref, lb_ref, u_ref, v_ref, *, width):
    h = _rmsnorm(x_ref[...], g_ref[...]).astype(BF16)
    v = _gelu(_mm(h, w_ref[:, width:]))
    mu = jnp.mean(v, axis=-1, keepdims=True)
    vc = v - mu
    var = jnp.mean(vc * vc, axis=-1, keepdims=True)
    vn = vc * lax.rsqrt(var + NORM_EPS) * lg_ref[...] + lb_ref[...]
    v_ref[...] = vn.astype(BF16)
    u_ref[...] = _gelu(_mm(h, w_ref[:, :width])).astype(BF16)


def _gmlp_in(x2, g, w_in, ln_g, ln_b, *, bm, casts):
    m, d = x2.shape
    width = w_in.shape[1] // 2
    row = lambda i: (i, 0)
    fixed = lambda i: (0, 0)
    est = (2 * bm * d * 4 + w_in.size * 2 + 2 * 2 * bm * width * 2
           + bm * d * 2 + 3 * bm * width * 4)
    return _call_with_casts(
        functools.partial(_gmlp_in_kernel, width=width),
        name="gmlp_in", grid=(m // bm,),
        in_specs=[
            pl.BlockSpec((bm, d), row),
            pl.BlockSpec((1, d), fixed),
            pl.BlockSpec(w_in.shape, fixed),
            pl.BlockSpec((1, width), fixed),
            pl.BlockSpec((1, width), fixed),
        ],
        out_specs=[pl.BlockSpec((bm, width), row), pl.BlockSpec((bm, width), row)],
        out_shape=[jax.ShapeDtypeStruct((m, width), BF16)] * 2,
        args=(x2, g, w_in, ln_g, ln_b), est=est, casts=casts)


def _gmlp_out_kernel(u_ref, v_ref, ws_ref, bs_ref, wout_ref, x_ref, o_ref, y_ref, *, groups):
    bm = u_ref.shape[0]
    n_chunks = bm // CHUNK
    for g in range(groups):
        cols = slice(g * LANES, (g + 1) * LANES)
        rhs = jnp.concatenate(
            [v_ref[c * CHUNK:(c + 1) * CHUNK, cols] for c in range(n_chunks)], axis=1)
        sg = _mm(ws_ref[g], rhs)
        bias = bs_ref[:, cols]
        for c in range(n_chunks):
            rows = slice(c * CHUNK, (c + 1) * CHUNK)
            s = sg[:, c * LANES:(c + 1) * LANES] + bias
            y_ref[rows, cols] = (u_ref[rows, cols].astype(F32) * s).astype(BF16)
    o_ref[...] = x_ref[...] + _mm(y_ref[...], wout_ref[...])


def _gmlp_out(u, v, w_s, b_full, w_out, x2, *, bm, casts):
    m, width = u.shape
    d = w_out.shape[1]
    groups = w_s.shape[0]
    row = lambda i: (i, 0)
    est = (2 * 2 * bm * width * 2 + w_s.size * 2 + b_full.size * 4 + w_out.size * 2
           + 2 * 2 * bm * d * 4 + bm * width * 2 + bm * d * 4)
    return _call_with_casts(
        functools.partial(_gmlp_out_kernel, groups=groups),
        name="gmlp_out", grid=(m // bm,),
        in_specs=[
            pl.BlockSpec((bm, width), row),
            pl.BlockSpec((bm, width), row),
            pl.BlockSpec(w_s.shape, lambda i: (0, 0, 0)),
            pl.BlockSpec(b_full.shape, lambda i: (0, 0)),
            pl.BlockSpec(w_out.shape, lambda i: (0, 0)),
            pl.BlockSpec((bm, d), row),
        ],
        out_specs=[pl.BlockSpec((bm, d), row)],
        out_shape=[jax.ShapeDtypeStruct((m, d), F32)],
        scratch_shapes=[pltpu.VMEM((bm, width), BF16)],
        args=(u, v, w_s, b_full, w_out, x2), est=est, casts=casts)


def _ffn_kernel(x_ref, g_ref, w1_ref, w2_ref, gf_ref, o_ref, h_ref, *, final_norm):
    j = pl.program_id(1)

    def chunk(h, acc):
        a = jnp.maximum(_mm(h, w1_ref[...]), 0.0)
        o_ref[...] = acc + _mm((a * a).astype(BF16), w2_ref[...])

    @pl.when(j == 0)
    def _():
        xf = x_ref[...]
        h_ref[...] = _rmsnorm(xf, g_ref[...]).astype(BF16)
        chunk(h_ref[...], xf)

    @pl.when(j > 0)
    def _():
        chunk(h_ref[...], o_ref[...])

    if final_norm:
        @pl.when(j == pl.num_programs(1) - 1)
        def _():
            o_ref[...] = _rmsnorm(o_ref[...], gf_ref[...])


def _ffn(x2, g, w1, w2, g_final, *, bm, fc, final_norm):
    m, d = x2.shape
    d_ff = w1.shape[1]
    est = (2 * bm * d * 4 + 2 * d * fc * 2 + 2 * fc * d * 2 + 2 * bm * d * 4
           + bm * d * 2 + bm * fc * 4 + bm * fc * 2 + bm * d * 4)
    return pl.pallas_call(
        functools.partial(_ffn_kernel, final_norm=final_norm),
        grid=(m // bm, d_ff // fc),
        in_specs=[
            pl.BlockSpec((bm, d), lambda i, j: (i, 0)),
            pl.BlockSpec((1, d), lambda i, j: (0, 0)),
            pl.BlockSpec((d, fc), lambda i, j: (0, j)),
            pl.BlockSpec((fc, d), lambda i, j: (j, 0)),
            pl.BlockSpec((1, d), lambda i, j: (0, 0)),
        ],
        out_specs=pl.BlockSpec((bm, d), lambda i, j: (i, 0)),
        out_shape=jax.ShapeDtypeStruct((m, d), F32),
        scratch_shapes=[pltpu.VMEM((bm, d), BF16)],
        compiler_params=pltpu.CompilerParams(
            dimension_semantics=("parallel", "arbitrary"),
            vmem_limit_bytes=_vmem_limit(est)),
        name="ffn_final" if final_norm else "ffn",
    )(x2, g, w1, w2, g_final)


QUARTER = HEAD_DIM // 4


def _swap_mid_quarters(t):
    lane = lax.broadcasted_iota(jnp.int32, t.shape, 1)
    from_right = (lane >= QUARTER) & (lane < 2 * QUARTER)
    from_left = (lane >= 2 * QUARTER) & (lane < 3 * QUARTER)
    return jnp.where(from_right, pltpu.roll(t, HEAD_DIM - QUARTER, axis=1),
                     jnp.where(from_left, pltpu.roll(t, QUARTER, axis=1), t))


def _cast_qkv_weights(src_ref, dst_ref, *, qk_width):
    for c0 in range(0, qk_width, HEAD_DIM):
        cols = slice(c0, c0 + HEAD_DIM)
        dst_ref[:, cols] = _swap_mid_quarters(src_ref[:, cols]).astype(BF16)
    dst_ref[:, qk_width:] = src_ref[:, qk_width:].astype(BF16)


def _qkv_kernel(x_ref, g_ref, w_ref, kn_ref, cos_ref, sin_ref,
                q_ref, k_ref, v_ref, *, n_heads, n_kv):
    h = _rmsnorm(x_ref[...], g_ref[...]).astype(BF16)
    q_w = n_heads * HEAD_DIM
    kv_w = n_kv * HEAD_DIM

    def partner(t):
        return pltpu.roll(t, HEAD_DIM // 2, axis=1)

    gain_b = _swap_mid_quarters(jnp.broadcast_to(kn_ref[...], cos_ref.shape))
    gc, gs = gain_b * cos_ref[...], partner(gain_b) * sin_ref[...]
    t = _mm(h, w_ref[:, q_w:q_w + kv_w])
    for hd in range(n_kv):
        cols = slice(hd * HEAD_DIM, (hd + 1) * HEAD_DIM)
        th = t[:, cols]
        r = lax.rsqrt(jnp.mean(th * th, axis=-1, keepdims=True) + NORM_EPS)
        k_ref[:, cols] = ((th * gc + partner(th) * gs) * r).astype(BF16)
    q_ref[...] = _mm(h, w_ref[:, :q_w]).astype(BF16)
    v_ref[...] = _mm(h, w_ref[:, q_w + kv_w:]).astype(BF16)


def _qkv(x2, g, w_qkv, k_norm, cos, sin_signed, *, bm, seq, casts):
    m, d = x2.shape
    n_kv = N_KV_HEADS
    kv_w = n_kv * HEAD_DIM
    q_w = w_qkv.shape[1] - 2 * kv_w
    n_heads = q_w // HEAD_DIM
    row = lambda i: (i, 0)
    fixed = lambda i: (0, 0)
    pos = lambda i: (i % (seq // bm), 0)
    est = (2 * bm * d * 4 + w_qkv.size * 2 + 4 * bm * HEAD_DIM * 4
           + 2 * bm * (q_w + 2 * kv_w) * 2 + bm * d * 2 + bm * q_w * 4 * 2)
    return _call_with_casts(
        functools.partial(_qkv_kernel, n_heads=n_heads, n_kv=n_kv),
        name="attn_qkv", grid=(m // bm,),
        in_specs=[
            pl.BlockSpec((bm, d), row),
            pl.BlockSpec((1, d), fixed),
            pl.BlockSpec(w_qkv.shape, fixed),
            pl.BlockSpec((1, HEAD_DIM), fixed),
            pl.BlockSpec((bm, HEAD_DIM), pos),
            pl.BlockSpec((bm, HEAD_DIM), pos),
        ],
        out_specs=[pl.BlockSpec((bm, q_w), row), pl.BlockSpec((bm, kv_w), row),
                   pl.BlockSpec((bm, kv_w), row)],
        out_shape=[jax.ShapeDtypeStruct((m, q_w), BF16),
                   jax.ShapeDtypeStruct((m, kv_w), BF16),
                   jax.ShapeDtypeStruct((m, kv_w), BF16)],
        args=(x2, g, w_qkv, k_norm, cos, sin_signed), est=est, casts=casts)


def _attn_kernel(q_ref, k_ref, v_ref, qn_ref, cos_ref, sin_ref, o_ref, gc_ref, gs_ref,
                 *, n_rep, rows, scale):
    qb = q_ref.shape[0]
    v = v_ref[...]
    v_ones = jnp.concatenate([v, jnp.ones_like(v)], axis=1)
    gain = _swap_mid_quarters(jnp.broadcast_to(qn_ref[...], (8, HEAD_DIM)))
    gc_ref[...] = cos_ref[...] * gain[0:1]
    gs_ref[...] = sin_ref[...] * pltpu.roll(gain, HEAD_DIM // 2, axis=1)[0:1]
    for r in range(n_rep):
        cols = slice(r * HEAD_DIM, (r + 1) * HEAD_DIM)
        for r0 in range(0, qb, rows):
            rs = slice(r0, r0 + rows)
            t = q_ref[rs, cols].astype(F32)
            rinv = lax.rsqrt(jnp.mean(t * t, axis=-1, keepdims=True) + NORM_EPS) * scale
            qh = (t * gc_ref[rs] + pltpu.roll(t, HEAD_DIM // 2, axis=1) * gs_ref[rs]) * rinv
            s = lax.dot_general(qh.astype(BF16), k_ref[...], (((1,), (1,)), ((), ())),
                                preferred_element_type=F32)
            p = jnp.exp2(s - jnp.max(s, axis=-1, keepdims=True))
            pv = _mm(p.astype(BF16), v_ones)
            o_ref[rs, cols] = (pv[:, :HEAD_DIM] / pv[:, HEAD_DIM:]).astype(BF16)


def _attention(q, k, v, q_norm, cos, sin_signed, *, batch, seq, qb, rows, casts):
    m, q_w = q.shape
    n_kv = k.shape[1] // HEAD_DIM
    n_rep = q_w // HEAD_DIM // n_kv
    nb = seq // qb
    est = (2 * 2 * qb * n_rep * HEAD_DIM * 2 + 2 * 2 * seq * HEAD_DIM * 2
           + seq * 2 * HEAD_DIM * 2 + 6 * qb * HEAD_DIM * 4 + 4 * rows * seq * (4 + 2))
    return _call_with_casts(
        functools.partial(_attn_kernel, n_rep=n_rep, rows=rows,
                          scale=HEAD_DIM ** -0.5 * math.log2(math.e)),
        name="attn_core", grid=(batch, n_kv, nb),
        in_specs=[
            pl.BlockSpec((qb, n_rep * HEAD_DIM), lambda b, g, i: (b * nb + i, g)),
            pl.BlockSpec((seq, HEAD_DIM), lambda b, g, i: (b, g)),
            pl.BlockSpec((seq, HEAD_DIM), lambda b, g, i: (b, g)),
            pl.BlockSpec((1, HEAD_DIM), lambda b, g, i: (0, 0)),
            pl.BlockSpec((qb, HEAD_DIM), lambda b, g, i: (i, 0)),
            pl.BlockSpec((qb, HEAD_DIM), lambda b, g, i: (i, 0)),
        ],
        out_specs=[pl.BlockSpec((qb, n_rep * HEAD_DIM), lambda b, g, i: (b * nb + i, g))],
        out_shape=[jax.ShapeDtypeStruct((m, q_w), BF16)],
        scratch_shapes=[pltpu.VMEM((qb, HEAD_DIM), F32), pltpu.VMEM((qb, HEAD_DIM), F32)],
        args=(q, k, v, q_norm, cos, sin_signed), est=est, casts=casts)


def _proj_kernel(a_ref, w_ref, x_ref, o_ref):
    o_ref[...] = x_ref[...] + _mm(a_ref[...], w_ref[...])


def _proj_residual(a, w, x2, *, bm):
    m, kdim = a.shape
    d = w.shape[1]
    row = lambda i: (i, 0)
    est = 2 * bm * kdim * 2 + w.size * 2 + 2 * 2 * bm * d * 4 + bm * d * 4
    return pl.pallas_call(
        _proj_kernel,
        grid=(m // bm,),
        in_specs=[pl.BlockSpec((bm, kdim), row), pl.BlockSpec(w.shape, lambda i: (0, 0)),
                  pl.BlockSpec((bm, d), row)],
        out_specs=pl.BlockSpec((bm, d), row),
        out_shape=jax.ShapeDtypeStruct((m, d), F32),
        compiler_params=pltpu.CompilerParams(
            dimension_semantics=("parallel",), vmem_limit_bytes=_vmem_limit(est)),
        name="attn_out",
    )(a, w, x2)


def _rope_tables(seq):
    axis_dim = HEAD_DIM // 2
    t = np.arange(seq)
    inv_freq = ROPE_THETA ** (-np.arange(0, axis_dim, 2, dtype=np.float64) / axis_dim)
    ang_r = (t // GRID_W)[:, None] * inv_freq[None, :]
    ang_c = (t % GRID_W)[:, None] * inv_freq[None, :]
    cr, sr, cc, sc = np.cos(ang_r), np.sin(ang_r), np.cos(ang_c), np.sin(ang_c)
    assert cr.shape == (seq, QUARTER)
    return (jnp.asarray(np.concatenate([cr, cc, cr, cc], axis=-1), dtype=F32),
            jnp.asarray(np.concatenate([-sr, -sc, sr, sc], axis=-1), dtype=F32))


def kernel(x, gm_w_in, gm_ln_g, gm_ln_b, gm_w_s, gm_b_s, gm_w_out, attn_w_qkv, attn_q_norm, attn_k_norm, attn_w_o, ffn_w1, ffn_w2, norm_mix, norm_ffn, norm_final):
    batch, seq, d = x.shape
    assert norm_mix.shape[0] == 2, "two layers: gMLP mixer then attention mixer"
    m = batch * seq
    width = gm_w_out.shape[1]
    x2 = x.reshape(m, d)
    cos, sin_signed = _rope_tables(seq)
    bm = 512
    fc = 2048

    (u, v), (w_out, w1_0, w2_0, w_qkv) = _gmlp_in(
        x2, norm_mix[0].reshape(1, d), gm_w_in[0].astype(BF16),
        gm_ln_g[0].reshape(1, width), gm_ln_b[0].reshape(1, width), bm=bm,
        casts=(_Cast(gm_w_out, 0), _Cast(ffn_w1, 0), _Cast(ffn_w2, 0),
               _Cast(attn_w_qkv, 0, functools.partial(
                   _cast_qkv_weights, qk_width=attn_w_qkv.shape[2] - N_KV_HEADS * HEAD_DIM))))
    b_full = jnp.repeat(gm_b_s[0].T, width // gm_b_s.shape[1], axis=1)
    (x2,), _ = _gmlp_out(u, v, gm_w_s[0].astype(BF16), b_full, w_out, x2, bm=bm, casts=())
    x2 = _ffn(x2, norm_ffn[0].reshape(1, d), w1_0, w2_0, norm_final.reshape(1, d),
              bm=bm, fc=fc, final_norm=False)

    (q, k, v), _ = _qkv(
        x2, norm_mix[1].reshape(1, d), w_qkv, attn_k_norm[0].reshape(1, HEAD_DIM),
        cos, sin_signed, bm=bm, seq=seq, casts=())
    (o,), (w_o, w1_1, w2_1) = _attention(
        q, k, v, attn_q_norm[0].reshape(1, HEAD_DIM), cos, sin_signed,
        batch=batch, seq=seq, qb=2048, rows=128,
        casts=(_Cast(attn_w_o, 0), _Cast(ffn_w1, 1), _Cast(ffn_w2, 1)))
    x2 = _proj_residual(o, w_o, x2, bm=bm)
    x2 = _ffn(x2, norm_ffn[1].reshape(1, d), w1_1, w2_1, norm_final.reshape(1, d),
              bm=bm, fc=fc, final_norm=True)
    return x2.reshape(batch, seq, d)
```

```python
import functools
import math
from typing import Callable, NamedTuple, Optional

import jax
import jax.numpy as jnp
import numpy as np
from jax import lax
from jax.experimental import pallas as pl
from jax.experimental.pallas import tpu as pltpu

NORM_EPS = 1e-6
CHUNK = 128
HEAD_DIM = 128
N_KV_HEADS = 4
GRID_W = 64
ROPE_THETA = 10000.0

V7X_VMEM_BYTES = 64 * 1024 * 1024
VMEM_RESERVE_BYTES = 6 * 1024 * 1024
VMEM_TEMP_BYTES = 8 * 1024 * 1024
LANES = 128

BF16 = jnp.bfloat16
F32 = jnp.float32


def _vmem_limit(estimate_bytes):
    return int(min(V7X_VMEM_BYTES - VMEM_RESERVE_BYTES, estimate_bytes + VMEM_TEMP_BYTES))


def _rmsnorm(xf, g):
    ms = jnp.mean(xf * xf, axis=-1, keepdims=True)
    return xf * lax.rsqrt(ms + NORM_EPS) * g


def _gelu(z):
    return 0.5 * z * (1.0 + lax.erf(z * math.sqrt(0.5)))


def _mm(a, b):
    return jnp.dot(a, b, preferred_element_type=F32)


class _Cast(NamedTuple):
    w: jax.Array
    layer: int
    transform: Optional[Callable] = None


def _call_with_casts(body, *, name, grid, in_specs, out_specs, out_shape, args,
                     est, casts=(), scratch_shapes=()):
    n_in, n_out, n_cast = len(in_specs), len(out_specs), len(casts)
    n_steps = math.prod(grid)

    def step(*idx):
        lin = idx[0]
        for i, n in zip(idx[1:], grid[1:]):
            lin = lin * n + i
        return lin

    cast_layers = [c.layer for c in casts]
    cast_shapes = [c.w.shape[1:] for c in casts]
    cast_fns = [c.transform for c in casts]
    casts = [c.w.reshape(-1, c.w.shape[2]) for c in casts]

    def kernel(*refs):
        ins, refs = refs[:n_in], refs[n_in:]
        cast_in, refs = refs[:n_cast], refs[n_cast:]
        outs, refs = refs[:n_out], refs[n_out:]
        cast_out, scratch = refs[:n_cast], refs[n_cast:]
        body(*ins, *outs, *scratch)
        for src, dst, fn in zip(cast_in, cast_out, cast_fns):
            if fn is None:
                dst[...] = src[...].astype(BF16)
            else:
                fn(src, dst)

    cast_in_specs = [
        pl.BlockSpec((r // n_steps, c),
                     functools.partial(lambda *idx, layer: (layer * n_steps + step(*idx), 0),
                                       layer=layer))
        for (r, c), layer in zip(cast_shapes, cast_layers)]
    cast_out_specs = [pl.BlockSpec((r // n_steps, c), lambda *idx: (step(*idx), 0))
                      for r, c in cast_shapes]
    est += sum(2 * (r // n_steps) * c * (4 + 2) for r, c in cast_shapes)
    outs = pl.pallas_call(
        kernel,
        grid=grid,
        in_specs=list(in_specs) + cast_in_specs,
        out_specs=list(out_specs) + cast_out_specs,
        out_shape=list(out_shape) + [jax.ShapeDtypeStruct(s, BF16) for s in cast_shapes],
        scratch_shapes=list(scratch_shapes),
        compiler_params=pltpu.CompilerParams(
            dimension_semantics=("parallel",) * len(grid), vmem_limit_bytes=_vmem_limit(est)),
        name=name,
    )(*args, *casts)
    return outs[:n_out], outs[n_out:]


def _gmlp_in_kernel(x_ref, g_ref, w_ref, lg_ref, lb_ref, v_ref):
    h = _rmsnorm(x_ref[...], g_ref[...]).astype(BF16)
    v = _gelu(_mm(h, w_ref[...]))
    mu = jnp.mean(v, axis=-1, keepdims=True)
    vc = v - mu
    var = jnp.mean(vc * vc, axis=-1, keepdims=True)
    vn = vc * lax.rsqrt(var + NORM_EPS) * lg_ref[...] + lb_ref[...]
    v_ref[...] = vn.astype(BF16)


def _gmlp_in(x2, g, w_v, ln_g, ln_b, *, bm, casts):
    m, d = x2.shape
    width = w_v.shape[1]
    row = lambda i: (i, 0)
    fixed = lambda i: (0, 0)
    est = (2 * bm * d * 4 + w_v.size * 2 + 2 * bm * width * 2
           + bm * d * 2 + 3 * bm * width * 4)
    return _call_with_casts(
        _gmlp_in_kernel,
        name="gmlp_in", grid=(m // bm,),
        in_specs=[
            pl.BlockSpec((bm, d), row),
            pl.BlockSpec((1, d), fixed),
            pl.BlockSpec(w_v.shape, fixed),
            pl.BlockSpec((1, width), fixed),
            pl.BlockSpec((1, width), fixed),
        ],
        out_specs=[pl.BlockSpec((bm, width), row)],
        out_shape=[jax.ShapeDtypeStruct((m, width), BF16)],
        args=(x2, g, w_v, ln_g, ln_b), est=est, casts=casts)


def _gmlp_out_kernel(v_ref, ws_ref, bs_ref, wout_ref, x_ref, g_ref, wu_ref, o_ref, y_ref,
                     *, groups):
    bm = v_ref.shape[0]
    n_chunks = bm // CHUNK
    xf = x_ref[...]
    u = _gelu(_mm(_rmsnorm(xf, g_ref[...]).astype(BF16), wu_ref[...]))
    for g in range(groups):
        cols = slice(g * LANES, (g + 1) * LANES)
        rhs = jnp.concatenate(
            [v_ref[c * CHUNK:(c + 1) * CHUNK, cols] for c in range(n_chunks)], axis=1)
        sg = _mm(ws_ref[g], rhs)
        bias = bs_ref[:, cols]
        for c in range(n_chunks):
            rows = slice(c * CHUNK, (c + 1) * CHUNK)
            s = sg[:, c * LANES:(c + 1) * LANES] + bias
            y_ref[rows, cols] = (u[rows, cols] * s).astype(BF16)
    o_ref[...] = xf + _mm(y_ref[...], wout_ref[...])


def _gmlp_out(v, w_s, b_full, w_out, x2, g, w_u, *, bm, casts):
    m, width = v.shape
    d = w_out.shape[1]
    groups = w_s.shape[0]
    row = lambda i: (i, 0)
    fixed = lambda i: (0, 0)
    est = (2 * bm * width * 2 + w_s.size * 2 + b_full.size * 4 + w_out.size * 2 + w_u.size * 2
           + 2 * 2 * bm * d * 4 + bm * width * 2 + bm * d * 2 + bm * width * 4)
    return _call_with_casts(
        functools.partial(_gmlp_out_kernel, groups=groups),
        name="gmlp_out", grid=(m // bm,),
        in_specs=[
            pl.BlockSpec((bm, width), row),
            pl.BlockSpec(w_s.shape, lambda i: (0, 0, 0)),
            pl.BlockSpec(b_full.shape, fixed),
            pl.BlockSpec(w_out.shape, fixed),
            pl.BlockSpec((bm, d), row),
            pl.BlockSpec((1, d), fixed),
            pl.BlockSpec(w_u.shape, fixed),
        ],
        out_specs=[pl.BlockSpec((bm, d), row)],
        out_shape=[jax.ShapeDtypeStruct((m, d), F32)],
        scratch_shapes=[pltpu.VMEM((bm, width), BF16)],
        args=(v, w_s, b_full, w_out, x2, g, w_u), est=est, casts=casts)


def _ffn_kernel(x_ref, g_ref, w1_ref, w2_ref, gf_ref, o_ref, h_ref, *, final_norm):
    j = pl.program_id(1)

    def chunk(h, acc):
        a = jnp.maximum(_mm(h, w1_ref[...]), 0.0)
        o_ref[...] = acc + _mm((a * a).astype(BF16), w2_ref[...])

    @pl.when(j == 0)
    def _():
        xf = x_ref[...]
        h_ref[...] = _rmsnorm(xf, g_ref[...]).astype(BF16)
        chunk(h_ref[...], xf)

    @pl.when(j > 0)
    def _():
        chunk(h_ref[...], o_ref[...])

    if final_norm:
        @pl.when(j == pl.num_programs(1) - 1)
        def _():
            o_ref[...] = _rmsnorm(o_ref[...], gf_ref[...])


def _ffn(x2, g, w1, w2, g_final, *, bm, fc, final_norm):
    m, d = x2.shape
    d_ff = w1.shape[1]
    est = (2 * bm * d * 4 + 2 * d * fc * 2 + 2 * fc * d * 2 + 2 * bm * d * 4
           + bm * d * 2 + bm * fc * 4 + bm * fc * 2 + bm * d * 4)
    return pl.pallas_call(
        functools.partial(_ffn_kernel, final_norm=final_norm),
        grid=(m // bm, d_ff // fc),
        in_specs=[
            pl.BlockSpec((bm, d), lambda i, j: (i, 0)),
            pl.BlockSpec((1, d), lambda i, j: (0, 0)),
            pl.BlockSpec((d, fc), lambda i, j: (0, j)),
            pl.BlockSpec((fc, d), lambda i, j: (j, 0)),
            pl.BlockSpec((1, d), lambda i, j: (0, 0)),
        ],
        out_specs=pl.BlockSpec((bm, d), lambda i, j: (i, 0)),
        out_shape=jax.ShapeDtypeStruct((m, d), F32),
        scratch_shapes=[pltpu.VMEM((bm, d), BF16)],
        compiler_params=pltpu.CompilerParams(
            dimension_semantics=("parallel", "arbitrary"),
            vmem_limit_bytes=_vmem_limit(est)),
        name="ffn_final" if final_norm else "ffn",
    )(x2, g, w1, w2, g_final)


QUARTER = HEAD_DIM // 4


def _swap_mid_quarters(t):
    lane = lax.broadcasted_iota(jnp.int32, t.shape, 1)
    from_right = (lane >= QUARTER) & (lane < 2 * QUARTER)
    from_left = (lane >= 2 * QUARTER) & (lane < 3 * QUARTER)
    return jnp.where(from_right, pltpu.roll(t, HEAD_DIM - QUARTER, axis=1),
                     jnp.where(from_left, pltpu.roll(t, QUARTER, axis=1), t))


def _cast_qkv_weights(src_ref, dst_ref, *, qk_width):
    for c0 in range(0, qk_width, HEAD_DIM):
        cols = slice(c0, c0 + HEAD_DIM)
        dst_ref[:, cols] = _swap_mid_quarters(src_ref[:, cols]).astype(BF16)
    dst_ref[:, qk_width:] = src_ref[:, qk_width:].astype(BF16)


def _qkv_kernel(x_ref, g_ref, w_ref, kn_ref, cos_ref, sin_ref,
                q_ref, k_ref, v_ref, *, n_heads, n_kv):
    h = _rmsnorm(x_ref[...], g_ref[...]).astype(BF16)
    q_w = n_heads * HEAD_DIM
    kv_w = n_kv * HEAD_DIM

    def partner(t):
        return pltpu.roll(t, HEAD_DIM // 2, axis=1)

    gain_b = _swap_mid_quarters(jnp.broadcast_to(kn_ref[...], cos_ref.shape))
    gc, gs = gain_b * cos_ref[...], partner(gain_b) * sin_ref[...]
    t = _mm(h, w_ref[:, q_w:q_w + kv_w])
    for hd in range(n_kv):
        cols = slice(hd * HEAD_DIM, (hd + 1) * HEAD_DIM)
        th = t[:, cols]
        r = lax.rsqrt(jnp.mean(th * th, axis=-1, keepdims=True) + NORM_EPS)
        k_ref[:, cols] = ((th * gc + partner(th) * gs) * r).astype(BF16)
    q_ref[...] = _mm(h, w_ref[:, :q_w]).astype(BF16)
    v_ref[...] = _mm(h, w_ref[:, q_w + kv_w:]).astype(BF16)


def _qkv(x2, g, w_qkv, k_norm, cos, sin_signed, *, bm, seq, casts):
    m, d = x2.shape
    n_kv = N_KV_HEADS
    kv_w = n_kv * HEAD_DIM
    q_w = w_qkv.shape[1] - 2 * kv_w
    n_heads = q_w // HEAD_DIM
    row = lambda i: (i, 0)
    fixed = lambda i: (0, 0)
    pos = lambda i: (i % (seq // bm), 0)
    est = (2 * bm * d * 4 + w_qkv.size * 2 + 4 * bm * HEAD_DIM * 4
           + 2 * bm * (q_w + 2 * kv_w) * 2 + bm * d * 2 + bm * q_w * 4 * 2)
    return _call_with_casts(
        functools.partial(_qkv_kernel, n_heads=n_heads, n_kv=n_kv),
        name="attn_qkv", grid=(m // bm,),
        in_specs=[
            pl.BlockSpec((bm, d), row),
            pl.BlockSpec((1, d), fixed),
            pl.BlockSpec(w_qkv.shape, fixed),
            pl.BlockSpec((1, HEAD_DIM), fixed),
            pl.BlockSpec((bm, HEAD_DIM), pos),
            pl.BlockSpec((bm, HEAD_DIM), pos),
        ],
        out_specs=[pl.BlockSpec((bm, q_w), row), pl.BlockSpec((bm, kv_w), row),
                   pl.BlockSpec((bm, kv_w), row)],
        out_shape=[jax.ShapeDtypeStruct((m, q_w), BF16),
                   jax.ShapeDtypeStruct((m, kv_w), BF16),
                   jax.ShapeDtypeStruct((m, kv_w), BF16)],
        args=(x2, g, w_qkv, k_norm, cos, sin_signed), est=est, casts=casts)


def _attn_kernel(q_ref, k_ref, v_ref, qn_ref, cos_ref, sin_ref, o_ref, gc_ref, gs_ref,
                 *, n_rep, rows, scale):
    qb = q_ref.shape[0]
    v = v_ref[...]
    v_ones = jnp.concatenate([v, jnp.ones_like(v)], axis=1)
    gain = _swap_mid_quarters(jnp.broadcast_to(qn_ref[...], (8, HEAD_DIM)))
    gc_ref[...] = cos_ref[...] * gain[0:1]
    gs_ref[...] = sin_ref[...] * pltpu.roll(gain, HEAD_DIM // 2, axis=1)[0:1]
    for r in range(n_rep):
        cols = slice(r * HEAD_DIM, (r + 1) * HEAD_DIM)
        for r0 in range(0, qb, rows):
            rs = slice(r0, r0 + rows)
            t = q_ref[rs, cols].astype(F32)
            rinv = lax.rsqrt(jnp.mean(t * t, axis=-1, keepdims=True) + NORM_EPS) * scale
            qh = (t * gc_ref[rs] + pltpu.roll(t, HEAD_DIM // 2, axis=1) * gs_ref[rs]) * rinv
            s = lax.dot_general(qh.astype(BF16), k_ref[...], (((1,), (1,)), ((), ())),
                                preferred_element_type=F32)
            p = jnp.exp2(s - jnp.max(s, axis=-1, keepdims=True))
            pv = _mm(p.astype(BF16), v_ones)
            o_ref[rs, cols] = (pv[:, :HEAD_DIM] / pv[:, HEAD_DIM:]).astype(BF16)


def _attention(q, k, v, q_norm, cos, sin_signed, *, batch, seq, qb, rows, casts):
    m, q_w = q.shape
    n_kv = k.shape[1] // HEAD_DIM
    n_rep = q_w // HEAD_DIM // n_kv
    nb = seq // qb
    est = (2 * 2 * qb * n_rep * HEAD_DIM * 2 + 2 * 2 * seq * HEAD_DIM * 2
           + seq * 2 * HEAD_DIM * 2 + 6 * qb * HEAD_DIM * 4 + 4 * rows * seq * (4 + 2))
    return _call_with_casts(
        functools.partial(_attn_kernel, n_rep=n_rep, rows=rows,
                          scale=HEAD_DIM ** -0.5 * math.log2(math.e)),
        name="attn_core", grid=(batch, n_kv, nb),
        in_specs=[
            pl.BlockSpec((qb, n_rep * HEAD_DIM), lambda b, g, i: (b * nb + i, g)),
            pl.BlockSpec((seq, HEAD_DIM), lambda b, g, i: (b, g)),
            pl.BlockSpec((seq, HEAD_DIM), lambda b, g, i: (b, g)),
            pl.BlockSpec((1, HEAD_DIM), lambda b, g, i: (0, 0)),
            pl.BlockSpec((qb, HEAD_DIM), lambda b, g, i: (i, 0)),
            pl.BlockSpec((qb, HEAD_DIM), lambda b, g, i: (i, 0)),
        ],
        out_specs=[pl.BlockSpec((qb, n_rep * HEAD_DIM), lambda b, g, i: (b * nb + i, g))],
        out_shape=[jax.ShapeDtypeStruct((m, q_w), BF16)],
        scratch_shapes=[pltpu.VMEM((qb, HEAD_DIM), F32), pltpu.VMEM((qb, HEAD_DIM), F32)],
        args=(q, k, v, q_norm, cos, sin_signed), est=est, casts=casts)


def _proj_kernel(a_ref, w_ref, x_ref, o_ref):
    o_ref[...] = x_ref[...] + _mm(a_ref[...], w_ref[...])


def _proj_residual(a, w, x2, *, bm):
    m, kdim = a.shape
    d = w.shape[1]
    row = lambda i: (i, 0)
    est = 2 * bm * kdim * 2 + w.size * 2 + 2 * 2 * bm * d * 4 + bm * d * 4
    return pl.pallas_call(
        _proj_kernel,
        grid=(m // bm,),
        in_specs=[pl.BlockSpec((bm, kdim), row), pl.BlockSpec(w.shape, lambda i: (0, 0)),
                  pl.BlockSpec((bm, d), row)],
        out_specs=pl.BlockSpec((bm, d), row),
        out_shape=jax.ShapeDtypeStruct((m, d), F32),
        compiler_params=pltpu.CompilerParams(
            dimension_semantics=("parallel",), vmem_limit_bytes=_vmem_limit(est)),
        name="attn_out",
    )(a, w, x2)


def _rope_tables(seq):
    axis_dim = HEAD_DIM // 2
    t = np.arange(seq)
    inv_freq = ROPE_THETA ** (-np.arange(0, axis_dim, 2, dtype=np.float64) / axis_dim)
    ang_r = (t // GRID_W)[:, None] * inv_freq[None, :]
    ang_c = (t % GRID_W)[:, None] * inv_freq[None, :]
    cr, sr, cc, sc = np.cos(ang_r), np.sin(ang_r), np.cos(ang_c), np.sin(ang_c)
    assert cr.shape == (seq, QUARTER)
    return (jnp.asarray(np.concatenate([cr, cc, cr, cc], axis=-1), dtype=F32),
            jnp.asarray(np.concatenate([-sr, -sc, sr, sc], axis=-1), dtype=F32))


def kernel(x, gm_w_in, gm_ln_g, gm_ln_b, gm_w_s, gm_b_s, gm_w_out, attn_w_qkv, attn_q_norm, attn_k_norm, attn_w_o, ffn_w1, ffn_w2, norm_mix, norm_ffn, norm_final):
    batch, seq, d = x.shape
    assert norm_mix.shape[0] == 2, "two layers: gMLP mixer then attention mixer"
    m = batch * seq
    width = gm_w_out.shape[1]
    x2 = x.reshape(m, d)
    cos, sin_signed = _rope_tables(seq)
    bm = 512
    fc = 2048

    g_mix0 = norm_mix[0].reshape(1, d)
    w_in = gm_w_in[0].astype(BF16)
    (v,), (w_out, w1_0, w_qkv) = _gmlp_in(
        x2, g_mix0, w_in[:, width:],
        gm_ln_g[0].reshape(1, width), gm_ln_b[0].reshape(1, width), bm=bm,
        casts=(_Cast(gm_w_out, 0), _Cast(ffn_w1, 0),
               _Cast(attn_w_qkv, 0, functools.partial(
                   _cast_qkv_weights, qk_width=attn_w_qkv.shape[2] - N_KV_HEADS * HEAD_DIM))))
    b_full = jnp.repeat(gm_b_s[0].T, width // gm_b_s.shape[1], axis=1)
    (x2,), (w2_0,) = _gmlp_out(v, gm_w_s[0].astype(BF16), b_full, w_out, x2, g_mix0,
                               w_in[:, :width], bm=bm, casts=(_Cast(ffn_w2, 0),))
    x2 = _ffn(x2, norm_ffn[0].reshape(1, d), w1_0, w2_0, norm_final.reshape(1, d),
              bm=bm, fc=fc, final_norm=False)

    (q, k, v), _ = _qkv(
        x2, norm_mix[1].reshape(1, d), w_qkv, attn_k_norm[0].reshape(1, HEAD_DIM),
        cos, sin_signed, bm=bm, seq=seq, casts=())
    (o,), (w_o, w1_1, w2_1) = _attention(
        q, k, v, attn_q_norm[0].reshape(1, HEAD_DIM), cos, sin_signed,
        batch=batch, seq=seq, qb=2048, rows=128,
        casts=(_Cast(attn_w_o, 0), _Cast(ffn_w1, 1), _Cast(ffn_w2, 1)))
    x2 = _proj_residual(o, w_o, x2, bm=bm)
    x2 = _ffn(x2, norm_ffn[1].reshape(1, d), w1_1, w2_1, norm_final.reshape(1, d),
              bm=bm, fc=fc, final_norm=True)
    return x2.reshape(batch, seq, d)
```

```python
import functools
import math
from typing import Callable, NamedTuple, Optional

import jax
import jax.numpy as jnp
import numpy as np
from jax import lax
from jax.experimental import pallas as pl
from jax.experimental.pallas import tpu as pltpu

NORM_EPS = 1e-6
CHUNK = 128
HEAD_DIM = 128
N_KV_HEADS = 4
GRID_W = 64
ROPE_THETA = 10000.0

V7X_VMEM_BYTES = 64 * 1024 * 1024
VMEM_RESERVE_BYTES = 6 * 1024 * 1024
VMEM_TEMP_BYTES = 8 * 1024 * 1024
LANES = 128

BF16 = jnp.bfloat16
F32 = jnp.float32


def _vmem_limit(estimate_bytes):
    return int(min(V7X_VMEM_BYTES - VMEM_RESERVE_BYTES, estimate_bytes + VMEM_TEMP_BYTES))


def _rmsnorm(xf, g):
    ms = jnp.mean(xf * xf, axis=-1, keepdims=True)
    return xf * lax.rsqrt(ms + NORM_EPS) * g


def _gelu(z):
    return 0.5 * z * (1.0 + lax.erf(z * math.sqrt(0.5)))


def _mm(a, b):
    return jnp.dot(a, b, preferred_element_type=F32)


class _Cast(NamedTuple):
    w: jax.Array
    layer: int
    transform: Optional[Callable] = None


def _call_with_casts(body, *, name, grid, in_specs, out_specs, out_shape, args,
                     est, casts=(), scratch_shapes=()):
    n_in, n_out, n_cast = len(in_specs), len(out_specs), len(casts)
    n_steps = math.prod(grid)

    def step(*idx):
        lin = idx[0]
        for i, n in zip(idx[1:], grid[1:]):
            lin = lin * n + i
        return lin

    cast_layers = [c.layer for c in casts]
    cast_shapes = [c.w.shape[1:] for c in casts]
    cast_fns = [c.transform for c in casts]
    casts = [c.w.reshape(-1, c.w.shape[2]) for c in casts]

    def kernel(*refs):
        ins, refs = refs[:n_in], refs[n_in:]
        cast_in, refs = refs[:n_cast], refs[n_cast:]
        outs, refs = refs[:n_out], refs[n_out:]
        cast_out, scratch = refs[:n_cast], refs[n_cast:]
        body(*ins, *outs, *scratch)
        for src, dst, fn in zip(cast_in, cast_out, cast_fns):
            if fn is None:
                dst[...] = src[...].astype(BF16)
            else:
                fn(src, dst)

    cast_in_specs = [
        pl.BlockSpec((r // n_steps, c),
                     functools.partial(lambda *idx, layer: (layer * n_steps + step(*idx), 0),
                                       layer=layer))
        for (r, c), layer in zip(cast_shapes, cast_layers)]
    cast_out_specs = [pl.BlockSpec((r // n_steps, c), lambda *idx: (step(*idx), 0))
                      for r, c in cast_shapes]
    est += sum(2 * (r // n_steps) * c * (4 + 2) for r, c in cast_shapes)
    outs = pl.pallas_call(
        kernel,
        grid=grid,
        in_specs=list(in_specs) + cast_in_specs,
        out_specs=list(out_specs) + cast_out_specs,
        out_shape=list(out_shape) + [jax.ShapeDtypeStruct(s, BF16) for s in cast_shapes],
        scratch_shapes=list(scratch_shapes),
        compiler_params=pltpu.CompilerParams(
            dimension_semantics=("parallel",) * len(grid), vmem_limit_bytes=_vmem_limit(est)),
        name=name,
    )(*args, *casts)
    return outs[:n_out], outs[n_out:]


def _gmlp_in_kernel(x_ref, g_ref, w_ref, lg_ref, lb_ref, u_ref, v_ref, *, width):
    h = _rmsnorm(x_ref[...], g_ref[...]).astype(BF16)
    v = _gelu(_mm(h, w_ref[:, width:]))
    mu = jnp.mean(v, axis=-1, keepdims=True)
    vc = v - mu
    var = jnp.mean(vc * vc, axis=-1, keepdims=True)
    vn = vc * lax.rsqrt(var + NORM_EPS) * lg_ref[...] + lb_ref[...]
    v_ref[...] = vn.astype(BF16)
    u_ref[...] = _gelu(_mm(h, w_ref[:, :width])).astype(BF16)


def _gmlp_in(x2, g, w_in, ln_g, ln_b, *, bm, casts):
    m, d = x2.shape
    width = w_in.shape[1] // 2
    row = lambda i: (i, 0)
    fixed = lambda i: (0, 0)
    est = (2 * bm * d * 4 + w_in.size * 2 + 2 * 2 * bm * width * 2
           + bm * d * 2 + 3 * bm * width * 4)
    return _call_with_casts(
        functools.partial(_gmlp_in_kernel, width=width),
        name="gmlp_in", grid=(m // bm,),
        in_specs=[
            pl.BlockSpec((bm, d), row),
            pl.BlockSpec((1, d), fixed),
            pl.BlockSpec(w_in.shape, fixed),
            pl.BlockSpec((1, width), fixed),
            pl.BlockSpec((1, width), fixed),
        ],
        out_specs=[pl.BlockSpec((bm, width), row), pl.BlockSpec((bm, width), row)],
        out_shape=[jax.ShapeDtypeStruct((m, width), BF16)] * 2,
        args=(x2, g, w_in, ln_g, ln_b), est=est, casts=casts)


def _gmlp_out_kernel(u_ref, v_ref, ws_ref, bs_ref, wout_ref, x_ref, o_ref, y_ref, *, groups):
    bm = u_ref.shape[0]
    n_chunks = bm // CHUNK
    for g in range(groups):
        cols = slice(g * LANES, (g + 1) * LANES)
        rhs = jnp.concatenate(
            [v_ref[c * CHUNK:(c + 1) * CHUNK, cols] for c in range(n_chunks)], axis=1)
        sg = _mm(ws_ref[g], rhs)
        bias = bs_ref[:, cols]
        for c in range(n_chunks):
            rows = slice(c * CHUNK, (c + 1) * CHUNK)
            s = sg[:, c * LANES:(c + 1) * LANES] + bias
            y_ref[rows, cols] = (u_ref[rows, cols].astype(F32) * s).astype(BF16)
    half = (groups // 2) * LANES
    o_ref[...] = (x_ref[...] + _mm(y_ref[:, :half], wout_ref[:half, :])
                  + _mm(y_ref[:, half:], wout_ref[half:, :]))


def _gmlp_out(u, v, w_s, b_full, w_out, x2, *, bm, casts):
    m, width = u.shape
    d = w_out.shape[1]
    groups = w_s.shape[0]
    row = lambda i: (i, 0)
    est = (2 * 2 * bm * width * 2 + w_s.size * 2 + b_full.size * 4 + w_out.size * 2
           + 2 * 2 * bm * d * 4 + bm * width * 2 + bm * d * 4)
    return _call_with_casts(
        functools.partial(_gmlp_out_kernel, groups=groups),
        name="gmlp_out", grid=(m // bm,),
        in_specs=[
            pl.BlockSpec((bm, width), row),
            pl.BlockSpec((bm, width), row),
            pl.BlockSpec(w_s.shape, lambda i: (0, 0, 0)),
            pl.BlockSpec(b_full.shape, lambda i: (0, 0)),
            pl.BlockSpec(w_out.shape, lambda i: (0, 0)),
            pl.BlockSpec((bm, d), row),
        ],
        out_specs=[pl.BlockSpec((bm, d), row)],
        out_shape=[jax.ShapeDtypeStruct((m, d), F32)],
        scratch_shapes=[pltpu.VMEM((bm, width), BF16)],
        args=(u, v, w_s, b_full, w_out, x2), est=est, casts=casts)


def _ffn_kernel(x_ref, g_ref, w1_ref, w2_ref, gf_ref, o_ref, h_ref, *, final_norm):
    j = pl.program_id(1)

    def chunk(h, acc):
        a = jnp.maximum(_mm(h, w1_ref[...]), 0.0)
        o_ref[...] = acc + _mm((a * a).astype(BF16), w2_ref[...])

    @pl.when(j == 0)
    def _():
        xf = x_ref[...]
        h_ref[...] = _rmsnorm(xf, g_ref[...]).astype(BF16)
        chunk(h_ref[...], xf)

    @pl.when(j > 0)
    def _():
        chunk(h_ref[...], o_ref[...])

    if final_norm:
        @pl.when(j == pl.num_programs(1) - 1)
        def _():
            o_ref[...] = _rmsnorm(o_ref[...], gf_ref[...])


def _ffn(x2, g, w1, w2, g_final, *, bm, fc, final_norm):
    m, d = x2.shape
    d_ff = w1.shape[1]
    est = (2 * bm * d * 4 + 2 * d * fc * 2 + 2 * fc * d * 2 + 2 * bm * d * 4
           + bm * d * 2 + bm * fc * 4 + bm * fc * 2 + bm * d * 4)
    return pl.pallas_call(
        functools.partial(_ffn_kernel, final_norm=final_norm),
        grid=(m // bm, d_ff // fc),
        in_specs=[
            pl.BlockSpec((bm, d), lambda i, j: (i, 0)),
            pl.BlockSpec((1, d), lambda i, j: (0, 0)),
            pl.BlockSpec((d, fc), lambda i, j: (0, j)),
            pl.BlockSpec((fc, d), lambda i, j: (j, 0)),
            pl.BlockSpec((1, d), lambda i, j: (0, 0)),
        ],
        out_specs=pl.BlockSpec((bm, d), lambda i, j: (i, 0)),
        out_shape=jax.ShapeDtypeStruct((m, d), F32),
        scratch_shapes=[pltpu.VMEM((bm, d), BF16)],
        compiler_params=pltpu.CompilerParams(
            dimension_semantics=("parallel", "arbitrary"),
            vmem_limit_bytes=_vmem_limit(est)),
        name="ffn_final" if final_norm else "ffn",
    )(x2, g, w1, w2, g_final)


QUARTER = HEAD_DIM // 4


def _swap_mid_quarters(t):
    lane = lax.broadcasted_iota(jnp.int32, t.shape, 1)
    from_right = (lane >= QUARTER) & (lane < 2 * QUARTER)
    from_left = (lane >= 2 * QUARTER) & (lane < 3 * QUARTER)
    return jnp.where(from_right, pltpu.roll(t, HEAD_DIM - QUARTER, axis=1),
                     jnp.where(from_left, pltpu.roll(t, QUARTER, axis=1), t))


def _cast_qkv_weights(src_ref, dst_ref, *, qk_width):
    for c0 in range(0, qk_width, HEAD_DIM):
        cols = slice(c0, c0 + HEAD_DIM)
        dst_ref[:, cols] = _swap_mid_quarters(src_ref[:, cols]).astype(BF16)
    dst_ref[:, qk_width:] = src_ref[:, qk_width:].astype(BF16)


def _qkv_kernel(x_ref, g_ref, w_ref, kn_ref, cos_ref, sin_ref,
                q_ref, k_ref, v_ref, *, n_heads, n_kv):
    h = _rmsnorm(x_ref[...], g_ref[...]).astype(BF16)
    q_w = n_heads * HEAD_DIM
    kv_w = n_kv * HEAD_DIM

    def partner(t):
        return pltpu.roll(t, HEAD_DIM // 2, axis=1)

    gain_b = _swap_mid_quarters(jnp.broadcast_to(kn_ref[...], cos_ref.shape))
    gc, gs = gain_b * cos_ref[...], partner(gain_b) * sin_ref[...]
    t = _mm(h, w_ref[:, q_w:q_w + kv_w])
    for hd in range(n_kv):
        cols = slice(hd * HEAD_DIM, (hd + 1) * HEAD_DIM)
        th = t[:, cols]
        r = lax.rsqrt(jnp.mean(th * th, axis=-1, keepdims=True) + NORM_EPS)
        k_ref[:, cols] = ((th * gc + partner(th) * gs) * r).astype(BF16)
    q_ref[...] = _mm(h, w_ref[:, :q_w]).astype(BF16)
    v_ref[...] = _mm(h, w_ref[:, q_w + kv_w:]).astype(BF16)


def _qkv(x2, g, w_qkv, k_norm, cos, sin_signed, *, bm, seq, casts):
    m, d = x2.shape
    n_kv = N_KV_HEADS
    kv_w = n_kv * HEAD_DIM
    q_w = w_qkv.shape[1] - 2 * kv_w
    n_heads = q_w // HEAD_DIM
    row = lambda i: (i, 0)
    fixed = lambda i: (0, 0)
    pos = lambda i: (i % (seq // bm), 0)
    est = (2 * bm * d * 4 + w_qkv.size * 2 + 4 * bm * HEAD_DIM * 4
           + 2 * bm * (q_w + 2 * kv_w) * 2 + bm * d * 2 + bm * q_w * 4 * 2)
    return _call_with_casts(
        functools.partial(_qkv_kernel, n_heads=n_heads, n_kv=n_kv),
        name="attn_qkv", grid=(m // bm,),
        in_specs=[
            pl.BlockSpec((bm, d), row),
            pl.BlockSpec((1, d), fixed),
            pl.BlockSpec(w_qkv.shape, fixed),
            pl.BlockSpec((1, HEAD_DIM), fixed),
            pl.BlockSpec((bm, HEAD_DIM), pos),
            pl.BlockSpec((bm, HEAD_DIM), pos),
        ],
        out_specs=[pl.BlockSpec((bm, q_w), row), pl.BlockSpec((bm, kv_w), row),
                   pl.BlockSpec((bm, kv_w), row)],
        out_shape=[jax.ShapeDtypeStruct((m, q_w), BF16),
                   jax.ShapeDtypeStruct((m, kv_w), BF16),
                   jax.ShapeDtypeStruct((m, kv_w), BF16)],
        args=(x2, g, w_qkv, k_norm, cos, sin_signed), est=est, casts=casts)


def _attn_kernel(q_ref, k_ref, v_ref, qn_ref, cos_ref, sin_ref, o_ref, gc_ref, gs_ref,
                 *, n_rep, rows, scale):
    qb = q_ref.shape[0]
    v = v_ref[...]
    v_ones = jnp.concatenate([v, jnp.ones_like(v)], axis=1)
    gain = _swap_mid_quarters(jnp.broadcast_to(qn_ref[...], (8, HEAD_DIM)))
    gc_ref[...] = cos_ref[...] * gain[0:1]
    gs_ref[...] = sin_ref[...] * pltpu.roll(gain, HEAD_DIM // 2, axis=1)[0:1]
    for r in range(n_rep):
        cols = slice(r * HEAD_DIM, (r + 1) * HEAD_DIM)
        for r0 in range(0, qb, rows):
            rs = slice(r0, r0 + rows)
            t = q_ref[rs, cols].astype(F32)
            rinv = lax.rsqrt(jnp.mean(t * t, axis=-1, keepdims=True) + NORM_EPS) * scale
            qh = (t * gc_ref[rs] + pltpu.roll(t, HEAD_DIM // 2, axis=1) * gs_ref[rs]) * rinv
            s = lax.dot_general(qh.astype(BF16), k_ref[...], (((1,), (1,)), ((), ())),
                                preferred_element_type=F32)
            p = jnp.exp2(s - jnp.max(s, axis=-1, keepdims=True))
            pv = _mm(p.astype(BF16), v_ones)
            o_ref[rs, cols] = (pv[:, :HEAD_DIM] / pv[:, HEAD_DIM:]).astype(BF16)


def _attention(q, k, v, q_norm, cos, sin_signed, *, batch, seq, qb, rows, casts):
    m, q_w = q.shape
    n_kv = k.shape[1] // HEAD_DIM
    n_rep = q_w // HEAD_DIM // n_kv
    nb = seq // qb
    est = (2 * 2 * qb * n_rep * HEAD_DIM * 2 + 2 * 2 * seq * HEAD_DIM * 2
           + seq * 2 * HEAD_DIM * 2 + 6 * qb * HEAD_DIM * 4 + 4 * rows * seq * (4 + 2))
    return _call_with_casts(
        functools.partial(_attn_kernel, n_rep=n_rep, rows=rows,
                          scale=HEAD_DIM ** -0.5 * math.log2(math.e)),
        name="attn_core", grid=(batch, n_kv, nb),
        in_specs=[
            pl.BlockSpec((qb, n_rep * HEAD_DIM), lambda b, g, i: (b * nb + i, g)),
            pl.BlockSpec((seq, HEAD_DIM), lambda b, g, i: (b, g)),
            pl.BlockSpec((seq, HEAD_DIM), lambda b, g, i: (b, g)),
            pl.BlockSpec((1, HEAD_DIM), lambda b, g, i: (0, 0)),
            pl.BlockSpec((qb, HEAD_DIM), lambda b, g, i: (i, 0)),
            pl.BlockSpec((qb, HEAD_DIM), lambda b, g, i: (i, 0)),
        ],
        out_specs=[pl.BlockSpec((qb, n_rep * HEAD_DIM), lambda b, g, i: (b * nb + i, g))],
        out_shape=[jax.ShapeDtypeStruct((m, q_w), BF16)],
        scratch_shapes=[pltpu.VMEM((qb, HEAD_DIM), F32), pltpu.VMEM((qb, HEAD_DIM), F32)],
        args=(q, k, v, q_norm, cos, sin_signed), est=est, casts=casts)


def _proj_kernel(a_ref, w_ref, x_ref, o_ref):
    o_ref[...] = x_ref[...] + _mm(a_ref[...], w_ref[...])


def _proj_residual(a, w, x2, *, bm):
    m, kdim = a.shape
    d = w.shape[1]
    row = lambda i: (i, 0)
    est = 2 * bm * kdim * 2 + w.size * 2 + 2 * 2 * bm * d * 4 + bm * d * 4
    return pl.pallas_call(
        _proj_kernel,
        grid=(m // bm,),
        in_specs=[pl.BlockSpec((bm, kdim), row), pl.BlockSpec(w.shape, lambda i: (0, 0)),
                  pl.BlockSpec((bm, d), row)],
        out_specs=pl.BlockSpec((bm, d), row),
        out_shape=jax.ShapeDtypeStruct((m, d), F32),
        compiler_params=pltpu.CompilerParams(
            dimension_semantics=("parallel",), vmem_limit_bytes=_vmem_limit(est)),
        name="attn_out",
    )(a, w, x2)


def _rope_tables(seq):
    axis_dim = HEAD_DIM // 2
    t = np.arange(seq)
    inv_freq = ROPE_THETA ** (-np.arange(0, axis_dim, 2, dtype=np.float64) / axis_dim)
    ang_r = (t // GRID_W)[:, None] * inv_freq[None, :]
    ang_c = (t % GRID_W)[:, None] * inv_freq[None, :]
    cr, sr, cc, sc = np.cos(ang_r), np.sin(ang_r), np.cos(ang_c), np.sin(ang_c)
    assert cr.shape == (seq, QUARTER)
    return (jnp.asarray(np.concatenate([cr, cc, cr, cc], axis=-1), dtype=F32),
            jnp.asarray(np.concatenate([-sr, -sc, sr, sc], axis=-1), dtype=F32))


def kernel(x, gm_w_in, gm_ln_g, gm_ln_b, gm_w_s, gm_b_s, gm_w_out, attn_w_qkv, attn_q_norm, attn_k_norm, attn_w_o, ffn_w1, ffn_w2, norm_mix, norm_ffn, norm_final):
    batch, seq, d = x.shape
    assert norm_mix.shape[0] == 2, "two layers: gMLP mixer then attention mixer"
    m = batch * seq
    width = gm_w_out.shape[1]
    x2 = x.reshape(m, d)
    cos, sin_signed = _rope_tables(seq)
    bm = 512
    fc = 2048

    (u, v), (w_out, w1_0, w2_0, w_qkv) = _gmlp_in(
        x2, norm_mix[0].reshape(1, d), gm_w_in[0].astype(BF16),
        gm_ln_g[0].reshape(1, width), gm_ln_b[0].reshape(1, width), bm=bm,
        casts=(_Cast(gm_w_out, 0), _Cast(ffn_w1, 0), _Cast(ffn_w2, 0),
               _Cast(attn_w_qkv, 0, functools.partial(
                   _cast_qkv_weights, qk_width=attn_w_qkv.shape[2] - N_KV_HEADS * HEAD_DIM))))
    b_full = jnp.repeat(gm_b_s[0].T, width // gm_b_s.shape[1], axis=1)
    (x2,), _ = _gmlp_out(u, v, gm_w_s[0].astype(BF16), b_full, w_out, x2, bm=bm, casts=())
    x2 = _ffn(x2, norm_ffn[0].reshape(1, d), w1_0, w2_0, norm_final.reshape(1, d),
              bm=bm, fc=fc, final_norm=False)

    (q, k, v), _ = _qkv(
        x2, norm_mix[1].reshape(1, d), w_qkv, attn_k_norm[0].reshape(1, HEAD_DIM),
        cos, sin_signed, bm=bm, seq=seq, casts=())
    (o,), (w_o, w1_1, w2_1) = _attention(
        q, k, v, attn_q_norm[0].reshape(1, HEAD_DIM), cos, sin_signed,
        batch=batch, seq=seq, qb=2048, rows=128,
        casts=(_Cast(attn_w_o, 0), _Cast(ffn_w1, 1), _Cast(ffn_w2, 1)))
    x2 = _proj_residual(o, w_o, x2, bm=bm)
    x2 = _ffn(x2, norm_ffn[1].reshape(1, d), w1_1, w2_1, norm_final.reshape(1, d),
              bm=bm, fc=fc, final_norm=True)
    return x2.reshape(batch, seq, d)
```
